```python
import math
import jax, jax.numpy as jnp
from jax import lax
import numpy as np

D_MODEL = 2048
BATCH = 4
SEQ = 2048
DEPTH = 2
DEC_BATCH = 128
DEC_SEQ = 1
PAST_LEN = 2048
PAGE_SIZE = 128

H_A = 8
N_A = 64
W_A = H_A * N_A
LORA_W = 64
LORA_A = 64
A_SHIFT_W = 3 * W_A + LORA_W + LORA_A
RWKV_SPLIT = (W_A, 2 * W_A, 3 * W_A, 3 * W_A + LORA_W)
H_B = 6
D_B = 64
W_B = H_B * 2 * D_B
H_C = 12
D_C = 64
W_C = H_C * D_C
MIX_W = W_A + W_B + W_C

Q_BLOCK = 128
NORM_EPS = 1e-6
SUBLN_EPS = 1e-5
GN_EPS = 64e-5

SPLIT_SIZES = (A_SHIFT_W, W_A,
               W_B, W_B, W_B, W_B,
               W_C, W_C, W_C, H_C, W_C,
               3 * D_MODEL)
SPLIT_POINTS = tuple(sum(SPLIT_SIZES[:i + 1]) for i in range(len(SPLIT_SIZES) - 1))
IN_COLS = sum(SPLIT_SIZES)

kernel_name = 'hybrid_rwkv7_diffattn_fox_step'


def rmsnorm(x, g, eps):
    xf = x.astype(jnp.float32)
    y = xf * lax.rsqrt(jnp.mean(xf * xf, axis=-1, keepdims=True) + eps)
    return (y * g.astype(jnp.float32)).astype(x.dtype)


def seg_softmax(scores, masks, scale, biases=None):
    parts = []
    for i, (s, m) in enumerate(zip(scores, masks)):
        s = s.astype(jnp.float32) * scale
        if biases is not None:
            s = s + biases[i]
        if m is not None:
            s = jnp.where(m, s, -jnp.inf)
        parts.append(s)
    bounds, acc = [], 0
    for s in parts[:-1]:
        acc += s.shape[-1]
        bounds.append(acc)
    p = jax.nn.softmax(jnp.concatenate(parts, axis=-1), axis=-1)
    return jnp.split(p, bounds, axis=-1)


def diff_attend(q, segs, lam):
    q1, q2 = q[..., :D_B], q[..., D_B:]
    masks = [m for _, _, m in segs]
    s1 = [jnp.einsum('bqhd,bkhd->bhqk', q1, k[..., :D_B]) for k, _, _ in segs]
    s2 = [jnp.einsum('bqhd,bkhd->bhqk', q2, k[..., D_B:]) for k, _, _ in segs]
    p1 = seg_softmax(s1, masks, D_B ** -0.5)
    p2 = seg_softmax(s2, masks, D_B ** -0.5)
    out = 0
    for a1, a2, (_, v, _) in zip(p1, p2, segs):
        out = out + jnp.einsum('bhqk,bkhd->bqhd', (a1 - lam * a2).astype(v.dtype), v)
    return out


def fox_attend(q, cum_q, segs):
    cq = jnp.moveaxis(cum_q, 2, 1)[..., :, None]
    scores = [jnp.einsum('bqhd,bkhd->bhqk', q, k) for k, _, _, _ in segs]
    biases = [cq - jnp.moveaxis(ck, 2, 1)[..., None, :] for _, _, ck, _ in segs]
    probs = seg_softmax(scores, [m for _, _, _, m in segs], D_C ** -0.5, biases)
    out = 0
    for pr, (_, v, _, _) in zip(probs, segs):
        out = out + jnp.einsum('bhqk,bkhd->bqhd', pr.astype(v.dtype), v)
    return out


def rwkv7_branch(u, shift0, s0, mu, w0, w2, a0, a2, k_k, k_a, r_k, gn_g, gn_b):
    f32 = jnp.float32
    B, T, _ = u.shape
    prev = jnp.concatenate([shift0[:, None, :].astype(u.dtype), u[:, :-1]], axis=1)
    m = u + mu * (prev - u)
    r, k, v, wd, ad = jnp.split(m, RWKV_SPLIT, axis=-1)
    w_log = -jax.nn.softplus(-(w0 + jnp.tanh(wd) @ w2).astype(f32)) - 0.5
    a = jax.nn.sigmoid((a0 + ad @ a2).astype(f32))
    heads = lambda t: t.astype(f32).reshape(B, T, H_A, N_A)
    r, k, v, a = heads(r), heads(k), heads(v), heads(a)
    decay = jnp.exp(-jnp.exp(heads(w_log)))
    kk = k * k_k.astype(f32).reshape(H_A, N_A)
    kk = kk * lax.rsqrt(jnp.maximum(jnp.sum(kk * kk, axis=-1, keepdims=True), 1e-24))
    k = k * (1.0 + (a - 1.0) * k_a.astype(f32).reshape(H_A, N_A))

    def step(S, inp):
        r_t, w_t, k_t, v_t, kk_t, a_t = inp
        sk = jnp.einsum('bhvk,bhk->bhv', S, kk_t)
        S = (S * w_t[:, :, None, :] - sk[..., None] * (kk_t * a_t)[:, :, None, :]
             + v_t[..., None] * k_t[:, :, None, :])
        return S, jnp.einsum('bhvk,bhk->bhv', S, r_t)

    xs = tuple(jnp.moveaxis(t, 1, 0) for t in (r, decay, k, v, kk, a))
    s_fin, y = lax.scan(step, s0.astype(f32), xs)
    y = jnp.moveaxis(y, 0, 1)
    mean = jnp.mean(y, axis=-1, keepdims=True)
    var = jnp.mean(jnp.square(y - mean), axis=-1, keepdims=True)
    y = ((y - mean) * lax.rsqrt(var + GN_EPS)).reshape(B, T, W_A)
    y = y * gn_g.astype(f32) + gn_b.astype(f32)
    bonus = jnp.sum(r * k * r_k.astype(f32), axis=-1, keepdims=True) * v
    y = y + bonus.reshape(B, T, W_A)
    return y.astype(u.dtype), s_fin, u[:, -1]


def trunk_layer(x, past, lam_init, p):
    (norm_g, w_in, mu, w0, w2, a0, a2, k_k, k_a, r_k, gn_g, gn_b,
     lq1, lk1, lq2, lk2, subln_g, b_f, w_branch, w_out) = p
    f32 = jnp.float32
    B, T, _ = x.shape
    h = rmsnorm(x, norm_g, NORM_EPS)
    (a_sh, a_gate, bq, bk, bv, b_gate, cq, ck, cv, cf, c_gate, m_gate) = jnp.split(
        h @ w_in, SPLIT_POINTS, axis=-1)
    bq = bq.reshape(B, T, H_B, 2 * D_B)
    bk = bk.reshape(B, T, H_B, 2 * D_B)
    bv = bv.reshape(B, T, H_B, 2 * D_B)
    cq = cq.reshape(B, T, H_C, D_C)
    ck = ck.reshape(B, T, H_C, D_C)
    cv = cv.reshape(B, T, H_C, D_C)
    logf = jax.nn.log_sigmoid((cf + b_f).astype(f32))
    lam = (jnp.exp(jnp.sum(lq1.astype(f32) * lk1.astype(f32)))
           - jnp.exp(jnp.sum(lq2.astype(f32) * lk2.astype(f32))) + lam_init)
    if past is None:
        s0 = jnp.zeros((B, H_A, N_A, N_A), f32)
        shift0 = jnp.zeros((B, A_SHIFT_W), x.dtype)
        cum = jnp.cumsum(logf, axis=1)
        ob, oc = [], []
        for q0 in range(0, T, Q_BLOCK):
            q1 = q0 + Q_BLOCK
            mask = jnp.arange(q1)[None, :] <= jnp.arange(q0, q1)[:, None]
            ob.append(diff_attend(bq[:, q0:q1], [(bk[:, :q1], bv[:, :q1], mask)], lam))
            oc.append(fox_attend(cq[:, q0:q1], cum[:, q0:q1],
                                 [(ck[:, :q1], cv[:, :q1], cum[:, :q1], mask)]))
        ob = jnp.concatenate(ob, axis=1)
        oc = jnp.concatenate(oc, axis=1)
    else:
        pk_b, pv_b, pk_c, pv_c, plogf, s0, shift0 = past
        n_past = pk_b.shape[1]
        cum = jnp.cumsum(jnp.concatenate([plogf.astype(f32), logf], axis=1), axis=1)
        cum_past, cum_new = cum[:, :n_past], cum[:, n_past:]
        mask = jnp.arange(T)[None, :] <= jnp.arange(T)[:, None]
        ob = diff_attend(bq, [(pk_b, pv_b, None), (bk, bv, mask)], lam)
        oc = fox_attend(cq, cum_new, [(pk_c, pv_c, cum_past, None), (ck, cv, cum_new, mask)])
    y_a, s_fin, shift_fin = rwkv7_branch(a_sh, shift0, s0, mu, w0, w2, a0, a2,
                                         k_k, k_a, r_k, gn_g, gn_b)
    y_b = (rmsnorm(ob, subln_g, SUBLN_EPS) * (1.0 - lam_init)).reshape(B, T, W_B)
    y_c = oc.reshape(B, T, W_C)
    y_a = y_a * jax.nn.silu(a_gate)
    y_b = y_b * jax.nn.silu(b_gate)
    y_c = y_c * jax.nn.silu(c_gate)
    g_a, g_b, g_c = jnp.split(m_gate, 3, axis=-1)
    merged = (jax.nn.sigmoid(g_a) * (y_a @ w_branch[:W_A])
              + jax.nn.sigmoid(g_b) * (y_b @ w_branch[W_A:W_A + W_B])
              + jax.nn.sigmoid(g_c) * (y_c @ w_branch[W_A + W_B:]))
    x = x + merged @ w_out
    return x, (bk, bv, ck, cv, logf.astype(x.dtype), s_fin.astype(x.dtype), shift_fin)


def setup_inputs(seed: int = 0) -> dict:
    key = jax.random.key(seed)
    ks = iter(jax.random.split(key, 48))
    f32 = jnp.float32
    nrm = lambda shape, scale=1.0: jax.random.normal(next(ks), shape, f32) * scale
    uni = lambda shape, lo, hi: jax.random.uniform(next(ks), shape, f32, lo, hi)
    n_pages = PAST_LEN // PAGE_SIZE
    n_used = DEC_BATCH * n_pages
    n_pool = n_used + max(1, n_used // 4)
    page_table = jax.random.permutation(next(ks), n_pool)[:n_used].reshape(
        DEC_BATCH, n_pages).astype(jnp.int32)
    return {
        'x_prompt': nrm((BATCH, SEQ, D_MODEL)),
        'x_sample': nrm((DEC_BATCH, DEC_SEQ, D_MODEL)),
        'cache_diff_k': nrm((DEPTH, n_pool, PAGE_SIZE, H_B, 2 * D_B)),
        'cache_diff_v': nrm((DEPTH, n_pool, PAGE_SIZE, H_B, 2 * D_B)),
        'cache_fox_k': nrm((DEPTH, n_pool, PAGE_SIZE, H_C, D_C)),
        'cache_fox_v': nrm((DEPTH, n_pool, PAGE_SIZE, H_C, D_C)),
        'cache_fox_logf': jax.nn.log_sigmoid(nrm((DEPTH, n_pool, PAGE_SIZE, H_C)) + 3.0),
        'state_rwkv_wkv': nrm((DEPTH, DEC_BATCH, H_A, N_A, N_A), 0.3),
        'state_rwkv_shift': nrm((DEPTH, DEC_BATCH, A_SHIFT_W)),
        'page_table': page_table,
        'norm_g': 1.0 + nrm((DEPTH, D_MODEL), 0.02),
        'w_in': nrm((DEPTH, D_MODEL, IN_COLS), D_MODEL ** -0.5),
        'rwkv_mu': uni((DEPTH, A_SHIFT_W), 0.0, 1.0),
        'rwkv_w0': nrm((DEPTH, W_A), 0.5),
        'rwkv_w2': nrm((DEPTH, LORA_W, W_A), LORA_W ** -0.5),
        'rwkv_a0': nrm((DEPTH, W_A), 0.5),
        'rwkv_a2': nrm((DEPTH, LORA_A, W_A), LORA_A ** -0.5),
        'rwkv_k_k': 0.85 + nrm((DEPTH, W_A), 0.05),
        'rwkv_k_a': 1.0 + nrm((DEPTH, W_A), 0.05),
        'rwkv_r_k': nrm((DEPTH, H_A, N_A), 0.1),
        'rwkv_gn_g': 1.0 + nrm((DEPTH, W_A), 0.02),
        'rwkv_gn_b': nrm((DEPTH, W_A), 0.02),
        'diff_lq1': nrm((DEPTH, D_B), 0.1),
        'diff_lk1': nrm((DEPTH, D_B), 0.1),
        'diff_lq2': nrm((DEPTH, D_B), 0.1),
        'diff_lk2': nrm((DEPTH, D_B), 0.1),
        'diff_subln_g': 1.0 + nrm((DEPTH, 2 * D_B), 0.02),
        'fox_b_f': uni((DEPTH, H_C), 1.0, 4.0),
        'w_branch': jnp.concatenate([nrm((DEPTH, W_A, D_MODEL), W_A ** -0.5),
                                     nrm((DEPTH, W_B, D_MODEL), W_B ** -0.5),
                                     nrm((DEPTH, W_C, D_MODEL), W_C ** -0.5)], axis=1),
        'w_out': nrm((DEPTH, D_MODEL, D_MODEL), D_MODEL ** -0.5),
        'final_g': 1.0 + nrm((D_MODEL,), 0.02),
    }


def reference(x_prompt, x_sample, cache_diff_k, cache_diff_v, cache_fox_k, cache_fox_v,
              cache_fox_logf, state_rwkv_wkv, state_rwkv_shift, page_table,
              norm_g, w_in, rwkv_mu, rwkv_w0, rwkv_w2, rwkv_a0, rwkv_a2, rwkv_k_k, rwkv_k_a,
              rwkv_r_k, rwkv_gn_g, rwkv_gn_b, diff_lq1, diff_lk1, diff_lq2, diff_lk2,
              diff_subln_g, fox_b_f, w_branch, w_out, final_g):
    n_seq, n_pages = page_table.shape

    def gather(c):
        g = c[page_table]
        return g.reshape((n_seq, n_pages * c.shape[1]) + c.shape[2:])

    hp, hs = x_prompt, x_sample
    st_p, st_s = [], []
    for l in range(DEPTH):
        lam_init = 0.8 - 0.6 * math.exp(-0.3 * l)
        p = (norm_g[l], w_in[l], rwkv_mu[l], rwkv_w0[l], rwkv_w2[l], rwkv_a0[l], rwkv_a2[l],
             rwkv_k_k[l], rwkv_k_a[l], rwkv_r_k[l], rwkv_gn_g[l], rwkv_gn_b[l],
             diff_lq1[l], diff_lk1[l], diff_lq2[l], diff_lk2[l], diff_subln_g[l], fox_b_f[l],
             w_branch[l], w_out[l])
        hp, sp = trunk_layer(hp, None, lam_init, p)
        past = (gather(cache_diff_k[l]), gather(cache_diff_v[l]), gather(cache_fox_k[l]),
                gather(cache_fox_v[l]), gather(cache_fox_logf[l]),
                state_rwkv_wkv[l], state_rwkv_shift[l])
        hs, ss = trunk_layer(hs, past, lam_init, p)
        st_p.append(sp)
        st_s.append(ss)
    y_prompt = rmsnorm(hp, final_g, NORM_EPS)
    y_sample = rmsnorm(hs, final_g, NORM_EPS)
    (dk_p, dv_p, fk_p, fv_p, lf_p, wkv_p, sh_p) = [jnp.stack([s[i] for s in st_p]) for i in range(7)]
    (dk_s, dv_s, fk_s, fv_s, lf_s, wkv_s, sh_s) = [jnp.stack([s[i] for s in st_s]) for i in range(7)]
    return (y_prompt, y_sample, dk_p, dk_s, dv_p, dv_s, fk_p, fk_s, fv_p, fv_s,
            lf_p, lf_s, wkv_p, wkv_s, sh_p, sh_s)
```

```python
import functools
import math

import jax
import jax.numpy as jnp
from jax import lax
from jax.experimental import pallas as pl
from jax.experimental.pallas import tpu as pltpu

F32 = jnp.float32
BF16 = jnp.bfloat16
HI = lax.Precision.HIGHEST

D_MODEL = 2048
DEPTH = 2
PAGE = 128
H_A, N_A = 8, 64
W_A = H_A * N_A
A_SHIFT_W = 3 * W_A + 128
H_B, D_B = 6, 64
W_B = H_B * 2 * D_B
H_C, D_C = 12, 64
W_C = H_C * D_C
NORM_EPS = 1e-6
SUBLN_EPS = 1e-5
GN_EPS = 64e-5
LANE = 128

C_GATES = 0
C_AGATE = 6144
C_U = 6656
C_BQ, C_BK, C_BV, C_BG = 8320, 9088, 9856, 10624
C_CQ, C_CK, C_CV, C_CG = 11392, 12160, 12928, 13696
C_CF = 14464
NP = 14592
TN_IN = 768

NEG = -1e30


def _cparams(sem, vmem_mb=48):
    return pltpu.CompilerParams(dimension_semantics=sem, vmem_limit_bytes=vmem_mb * 1024 * 1024)


def _softplus(z):
    return jnp.maximum(z, 0.0) + jnp.log1p(jnp.exp(-jnp.abs(z)))


def _sigmoid(z):
    return jax.nn.sigmoid(z)


def _silu(z):
    return z * jax.nn.sigmoid(z)


def _dot_nt(a, b):
    return lax.dot_general(a, b, (((1,), (1,)), ((), ())), preferred_element_type=F32)


def _inproj_kernel(x_ref, g_ref, w_ref, o_ref, h_ref):
    @pl.when(pl.program_id(1) == 0)
    def _():
        x = x_ref[...]
        ms = jnp.mean(x * x, axis=-1, keepdims=True)
        h_ref[...] = (x * lax.rsqrt(ms + NORM_EPS) * g_ref[...]).astype(BF16)

    o_ref[...] = jnp.dot(h_ref[...], w_ref[...], preferred_element_type=F32)


def _inproj(x2d, g, w_p, tm):
    m = x2d.shape[0]
    return pl.pallas_call(
        _inproj_kernel,
        grid=(m // tm, NP // TN_IN),
        in_specs=[pl.BlockSpec((tm, D_MODEL), lambda i, j: (i, 0)),
                  pl.BlockSpec((1, D_MODEL), lambda i, j: (0, 0)),
                  pl.BlockSpec((D_MODEL, TN_IN), lambda i, j: (0, j))],
        out_specs=pl.BlockSpec((tm, TN_IN), lambda i, j: (i, j)),
        out_shape=jax.ShapeDtypeStruct((m, NP), F32),
        scratch_shapes=[pltpu.VMEM((tm, D_MODEL), BF16)],
        compiler_params=_cparams(("parallel", "arbitrary")),
        name="inproj",
    )(x2d, g.reshape(1, D_MODEL), w_p)


def _rwkv_prep_kernel(u_ref, prev_ref, mu_ref, w0_ref, a0_ref, kk_ref, ka_ref, rk_ref,
                      w2_ref, a2_ref, bd_ref,
                      r_o, w_o, k_o, v_o, kk_o, b_o, bonus_o):
    u = u_ref[...]
    m = u + mu_ref[...] * (prev_ref[...] - u)
    r = m[:, 0:W_A]
    k = m[:, W_A:2 * W_A]
    v = m[:, 2 * W_A:3 * W_A]
    lo = m[:, 3 * W_A:3 * W_A + 128]
    zw = w0_ref[...] + jnp.dot(jnp.tanh(lo).astype(BF16), w2_ref[...], preferred_element_type=F32)
    w_log = -_softplus(-zw) - 0.5
    a = _sigmoid(a0_ref[...] + jnp.dot(lo.astype(BF16), a2_ref[...], preferred_element_type=F32))
    decay = jnp.exp(-jnp.exp(w_log))
    bd = bd_ref[...]
    kk = k * kk_ref[...]
    ss = jnp.dot(kk * kk, bd, precision=HI, preferred_element_type=F32)
    kk = kk * lax.rsqrt(jnp.maximum(ss, 1e-24))
    k2 = k * (1.0 + (a - 1.0) * ka_ref[...])
    bonus = jnp.dot(r * k2 * rk_ref[...], bd, precision=HI, preferred_element_type=F32) * v
    r_o[...] = r
    w_o[...] = decay
    k_o[...] = k2
    v_o[...] = v
    kk_o[...] = kk
    b_o[...] = kk * a
    bonus_o[...] = bonus


def _rwkv_prep(proj, prev, mu, w0, a0, k_k, k_a, r_k, w2p, a2p, bd, tm):
    m = proj.shape[0]
    row = lambda n: pl.BlockSpec((1, n), lambda i: (0, 0))
    full = lambda a: pl.BlockSpec(a.shape, lambda i: (0, 0))
    tok = pl.BlockSpec((tm, W_A), lambda i: (i, 0))
    return pl.pallas_call(
        _rwkv_prep_kernel,
        grid=(m // tm,),
        in_specs=[pl.BlockSpec((tm, A_SHIFT_W), lambda i: (i, C_U // A_SHIFT_W)),
                  pl.BlockSpec((tm, A_SHIFT_W), lambda i: (i, 0)),
                  row(A_SHIFT_W), row(W_A), row(W_A), row(W_A), row(W_A), row(W_A),
                  full(w2p), full(a2p), full(bd)],
        out_specs=[tok] * 7,
        out_shape=[jax.ShapeDtypeStruct((m, W_A), F32)] * 7,
        compiler_params=_cparams(("parallel",)),
        name="rwkv_prep",
    )(proj, prev, mu.reshape(1, -1), w0.reshape(1, -1), a0.reshape(1, -1), k_k.reshape(1, -1),
      k_a.reshape(1, -1), r_k.reshape(1, -1), w2p, a2p, bd)


V_HI = 16
SCAN_TC = 16


def _rwkv_scan_kernel(w_ref, kk_ref, b_ref, k_ref, r_ref, v_ref, s0_ref, y_ref, s_ref):
    @pl.when(pl.program_id(0) == 0)
    def _():
        s_ref[...] = s0_ref[...]

    def step(t, carry):
        w = w_ref[t]
        kk = kk_ref[t]
        b = b_ref[t]
        kv = k_ref[t]
        r = r_ref[t]
        vt = v_ref[t]
        rows = []
        for vh in range(V_HI):
            s = s_ref[vh]
            sk = jnp.sum(s * kk, axis=0, keepdims=True)
            s2 = s * w - sk * b + vt[vh:vh + 1, :] * kv
            s_ref[vh] = s2
            rows.append(jnp.sum(s2 * r, axis=0, keepdims=True))
        y_ref[t] = jnp.concatenate(rows, axis=0)
        return carry

    lax.fori_loop(0, SCAN_TC, step, 0)


def _rwkv_scan(w_t, kk_t, b_t, k_t, r_t, v_t, s0):
    t = w_t.shape[0]
    tile = pl.BlockSpec((SCAN_TC, N_A, LANE), lambda i: (i, 0, 0))
    vspec = pl.BlockSpec((SCAN_TC, V_HI, LANE), lambda i: (i, 0, 0))
    sspec = pl.BlockSpec((V_HI, N_A, LANE), lambda i: (0, 0, 0))
    return pl.pallas_call(
        _rwkv_scan_kernel,
        grid=(t // SCAN_TC,),
        in_specs=[tile] * 5 + [vspec, sspec],
        out_specs=[vspec, sspec],
        out_shape=[jax.ShapeDtypeStruct((t, V_HI, LANE), F32),
                   jax.ShapeDtypeStruct((V_HI, N_A, LANE), F32)],
        compiler_params=_cparams(("arbitrary",)),
        name="rwkv_scan",
    )(w_t, kk_t, b_t, k_t, r_t, v_t, s0)


def _to_scan_tiles(x, bsz, t):
    x = x.reshape(bsz, t, H_A, N_A).transpose(1, 3, 0, 2).reshape(t, N_A, 1, bsz * H_A)
    return jnp.broadcast_to(x, (t, N_A, LANE // (bsz * H_A), bsz * H_A)).reshape(t, N_A, LANE)


def _to_scan_v(v, bsz, t):
    v = v.reshape(bsz, t, H_A, V_HI, N_A // V_HI).transpose(1, 3, 4, 0, 2)
    return v.reshape(t, V_HI, LANE)


def _from_scan_y(y, bsz, t):
    y = y.reshape(t, V_HI, N_A // V_HI, bsz, H_A).transpose(3, 0, 4, 1, 2)
    return y.reshape(bsz * t, W_A)


def _from_scan_state(s, bsz):
    s = s.reshape(V_HI, N_A, N_A // V_HI, bsz, H_A).transpose(3, 4, 0, 2, 1)
    return s.reshape(bsz, H_A, N_A, N_A)


def _rwkv_step_kernel(s_ref, w_ref, kk_ref, b_ref, k_ref, r_ref, v_ref, so_ref, y_ref):
    s = s_ref[...]
    sk = jnp.sum(s * kk_ref[...], axis=-1, keepdims=True)
    s2 = s * w_ref[...] - sk * b_ref[...] + v_ref[...] * k_ref[...]
    so_ref[...] = s2
    y_ref[...] = jnp.sum(s2 * r_ref[...], axis=-1, keepdims=True)


def _rwkv_step(state_all, layer, w, kk, b, k2, r, v, bb=8):
    n = w.shape[0]
    rowv = lambda x: x.reshape(n, H_A, 1, N_A)
    rspec = pl.BlockSpec((bb, H_A, 1, N_A), lambda i: (i, 0, 0, 0))
    cspec = pl.BlockSpec((bb, H_A, N_A, 1), lambda i: (i, 0, 0, 0))
    sspec = pl.BlockSpec((bb, H_A, N_A, N_A), lambda i: (i, 0, 0, 0))
    return pl.pallas_call(
        _rwkv_step_kernel,
        grid=(n // bb,),
        in_specs=[pl.BlockSpec((None, bb, H_A, N_A, N_A), lambda i: (layer, i, 0, 0, 0)),
                  rspec, rspec, rspec, rspec, rspec, cspec],
        out_specs=[sspec, cspec],
        out_shape=[jax.ShapeDtypeStruct((n, H_A, N_A, N_A), F32),
                   jax.ShapeDtypeStruct((n, H_A, N_A, 1), F32)],
        compiler_params=_cparams(("parallel",)),
        name="rwkv_step",
    )(state_all, rowv(w), rowv(kk), rowv(b), rowv(k2), rowv(r), v.reshape(n, H_A, N_A, 1))


def _lam(lq1, lk1, lq2, lk2, lam_init):
    return (jnp.exp(jnp.sum(lq1[...] * lk1[...], axis=-1, keepdims=True))
            - jnp.exp(jnp.sum(lq2[...] * lk2[...], axis=-1, keepdims=True)) + lam_init)


def _online(s, m, l):
    m_new = jnp.maximum(m, jnp.max(s, axis=-1, keepdims=True))
    alpha = jnp.exp(m - m_new)
    p = jnp.exp(s - m_new)
    return p, alpha, m_new, l * alpha + jnp.sum(p, axis=-1, keepdims=True)


def _diff_attn_kernel(q_ref, k_ref, v_ref, g_ref, sg_ref, lq1, lk1, lq2, lk2, o_ref,
                      *, lam_init, tq, tk):
    qi = pl.program_id(2)
    lane = lax.broadcasted_iota(jnp.int32, (1, LANE), 1)
    q = q_ref[...] * (D_B ** -0.5)
    q1 = jnp.where(lane < D_B, q, 0.0).astype(BF16)
    q2 = jnp.where(lane >= D_B, q, 0.0).astype(BF16)
    rows = qi * tq + lax.broadcasted_iota(jnp.int32, (tq, tk), 0)
    cols0 = lax.broadcasted_iota(jnp.int32, (tq, tk), 1)

    def body(j, carry):
        m1, l1, a1, m2, l2, a2 = carry
        off = pl.multiple_of(j * tk, tk)
        kb = k_ref[pl.ds(off, tk), :].astype(BF16)
        vb = v_ref[pl.ds(off, tk), :].astype(BF16)
        mask = (cols0 + off) <= rows
        s1 = jnp.where(mask, _dot_nt(q1, kb), NEG)
        s2 = jnp.where(mask, _dot_nt(q2, kb), NEG)
        p1, al1, m1, l1 = _online(s1, m1, l1)
        p2, al2, m2, l2 = _online(s2, m2, l2)
        a1 = a1 * al1 + jnp.dot(p1.astype(BF16), vb, preferred_element_type=F32)
        a2 = a2 * al2 + jnp.dot(p2.astype(BF16), vb, preferred_element_type=F32)
        return m1, l1, a1, m2, l2, a2

    mi = jnp.full((tq, 1), NEG, F32)
    zi = jnp.zeros((tq, 1), F32)
    ai = jnp.zeros((tq, LANE), F32)
    nkv = (qi * tq + tq + tk - 1) // tk
    m1, l1, a1, m2, l2, a2 = lax.fori_loop(0, nkv, body, (mi, zi, ai, mi, zi, ai))
    lam = _lam(lq1, lk1, lq2, lk2, lam_init)
    o = a1 / l1 - lam * (a2 / l2)
    o = o * lax.rsqrt(jnp.mean(o * o, axis=-1, keepdims=True) + SUBLN_EPS) * sg_ref[...]
    o_ref[...] = o * (1.0 - lam_init) * _silu(g_ref[...])


def _diff_attn(proj, bsz, t, subln_g, lq1, lk1, lq2, lk2, lam_init, tq=256, tk=256):
    nq = t // tq
    small = pl.BlockSpec((1, D_B), lambda b, h, i: (0, 0))
    return pl.pallas_call(
        functools.partial(_diff_attn_kernel, lam_init=lam_init, tq=tq, tk=tk),
        grid=(bsz, H_B, nq),
        in_specs=[pl.BlockSpec((tq, LANE), lambda b, h, i: (b * nq + i, C_BQ // LANE + h)),
                  pl.BlockSpec((t, LANE), lambda b, h, i: (b, C_BK // LANE + h)),
                  pl.BlockSpec((t, LANE), lambda b, h, i: (b, C_BV // LANE + h)),
                  pl.BlockSpec((tq, LANE), lambda b, h, i: (b * nq + i, C_BG // LANE + h)),
                  pl.BlockSpec((1, LANE), lambda b, h, i: (0, 0)),
                  small, small, small, small],
        out_specs=pl.BlockSpec((tq, LANE), lambda b, h, i: (b * nq + i, h)),
        out_shape=jax.ShapeDtypeStruct((bsz * t, W_B), F32),
        compiler_params=_cparams(("parallel", "parallel", "arbitrary")),
        name="diff_attn",
    )(proj, proj, proj, proj, subln_g.reshape(1, -1), lq1.reshape(1, -1), lk1.reshape(1, -1),
      lq2.reshape(1, -1), lk2.reshape(1, -1))


def _foxcum_kernel(cf_ref, bf_ref, tri_ref, lf_ref, cum_ref, cumt_ref, carry_ref, *, tm):
    @pl.when(pl.program_id(1) == 0)
    def _():
        carry_ref[...] = jnp.zeros_like(carry_ref)

    lf = -_softplus(-(cf_ref[...] + bf_ref[...]))
    lf_ref[...] = lf
    c = jnp.dot(tri_ref[...], lf, precision=HI, preferred_element_type=F32) + carry_ref[...]
    cum_ref[...] = c
    carry_ref[...] = c[tm - 1:tm, :]
    cumt_ref[...] = c.T[:16, :]


def _foxcum(proj, bsz, t, bf_pad, tm):
    nt = t // tm
    tri = (jnp.arange(tm)[:, None] >= jnp.arange(tm)[None, :]).astype(F32)
    tok = pl.BlockSpec((tm, LANE), lambda b, i: (b * nt + i, 0))
    return pl.pallas_call(
        functools.partial(_foxcum_kernel, tm=tm),
        grid=(bsz, nt),
        in_specs=[pl.BlockSpec((tm, LANE), lambda b, i: (b * nt + i, C_CF // LANE)),
                  pl.BlockSpec((1, LANE), lambda b, i: (0, 0)),
                  pl.BlockSpec((tm, tm), lambda b, i: (0, 0))],
        out_specs=[tok, tok, pl.BlockSpec((None, 16, tm), lambda b, i: (b, 0, i))],
        out_shape=[jax.ShapeDtypeStruct((bsz * t, LANE), F32),
                   jax.ShapeDtypeStruct((bsz * t, LANE), F32),
                   jax.ShapeDtypeStruct((bsz, 16, t), F32)],
        scratch_shapes=[pltpu.VMEM((1, LANE), F32)],
        compiler_params=_cparams(("parallel", "arbitrary")),
        name="fox_cum",
    )(proj, bf_pad, tri)


def _fox_attn_kernel(q_ref, k_ref, v_ref, g_ref, cum_ref, cumt_ref, o_ref, *, tq, tk):
    hp = pl.program_id(1)
    qi = pl.program_id(2)
    lane = lax.broadcasted_iota(jnp.int32, (1, LANE), 1)
    q = q_ref[...] * (D_C ** -0.5)
    qa = jnp.where(lane < D_C, q, 0.0).astype(BF16)
    qb = jnp.where(lane >= D_C, q, 0.0).astype(BF16)
    cq = cum_ref[...]
    cqa = jnp.sum(jnp.where(lane == 2 * hp, cq, 0.0), axis=-1, keepdims=True)
    cqb = jnp.sum(jnp.where(lane == 2 * hp + 1, cq, 0.0), axis=-1, keepdims=True)
    rows = qi * tq + lax.broadcasted_iota(jnp.int32, (tq, tk), 0)
    cols0 = lax.broadcasted_iota(jnp.int32, (tq, tk), 1)

    def body(j, carry):
        ma, la, aa, mb, lb, ab = carry
        off = pl.multiple_of(j * tk, tk)
        kb = k_ref[pl.ds(off, tk), :].astype(BF16)
        vb = v_ref[pl.ds(off, tk), :].astype(BF16)
        cka = cumt_ref[pl.ds(2 * hp, 1), pl.ds(off, tk)]
        ckb = cumt_ref[pl.ds(2 * hp + 1, 1), pl.ds(off, tk)]
        mask = (cols0 + off) <= rows
        sa = jnp.where(mask, _dot_nt(qa, kb) + (cqa - cka), NEG)
        sb = jnp.where(mask, _dot_nt(qb, kb) + (cqb - ckb), NEG)
        pa, ala, ma, la = _online(sa, ma, la)
        pb, alb, mb, lb = _online(sb, mb, lb)
        aa = aa * ala + jnp.dot(pa.astype(BF16), vb, preferred_element_type=F32)
        ab = ab * alb + jnp.dot(pb.astype(BF16), vb, preferred_element_type=F32)
        return ma, la, aa, mb, lb, ab

    mi = jnp.full((tq, 1), NEG, F32)
    zi = jnp.zeros((tq, 1), F32)
    ai = jnp.zeros((tq, LANE), F32)
    nkv = (qi * tq + tq + tk - 1) // tk
    ma, la, aa, mb, lb, ab = lax.fori_loop(0, nkv, body, (mi, zi, ai, mi, zi, ai))
    o = jnp.where(lane < D_C, aa / la, ab / lb)
    o_ref[...] = o * _silu(g_ref[...])


def _fox_attn(proj, cum, cumt, bsz, t, tq=256, tk=256):
    nq = t // tq
    return pl.pallas_call(
        functools.partial(_fox_attn_kernel, tq=tq, tk=tk),
        grid=(bsz, H_C // 2, nq),
        in_specs=[pl.BlockSpec((tq, LANE), lambda b, h, i: (b * nq + i, C_CQ // LANE + h)),
                  pl.BlockSpec((t, LANE), lambda b, h, i: (b, C_CK // LANE + h)),
                  pl.BlockSpec((t, LANE), lambda b, h, i: (b, C_CV // LANE + h)),
                  pl.BlockSpec((tq, LANE), lambda b, h, i: (b * nq + i, C_CG // LANE + h)),
                  pl.BlockSpec((tq, LANE), lambda b, h, i: (b * nq + i, 0)),
                  pl.BlockSpec((None, 16, t), lambda b, h, i: (b, 0, 0))],
        out_specs=pl.BlockSpec((tq, LANE), lambda b, h, i: (b * nq + i, h)),
        out_shape=jax.ShapeDtypeStruct((bsz * t, W_C), F32),
        compiler_params=_cparams(("parallel", "parallel", "arbitrary")),
        name="fox_attn",
    )(proj, proj, proj, proj, cum, cumt)


PP = 4
N_PAGES = 16
NSTEP = N_PAGES // PP


def _diff_dec_kernel(pt_ref, q_ref, kn_ref, vn_ref, g_ref, sg_ref, lq1, lk1, lq2, lk2, *rest,
                     lam_init):
    kp = rest[:PP]
    vp = rest[PP:2 * PP]
    o_ref, m_ref, l_ref, acc_ref = rest[2 * PP:]
    b = pl.program_id(0)
    s = pl.program_id(1)

    @pl.when(s == 0)
    def _():
        m_ref[...] = jnp.full_like(m_ref, NEG)
        l_ref[...] = jnp.zeros_like(l_ref)
        acc_ref[...] = jnp.zeros_like(acc_ref)

    qrow = q_ref[...] * (D_B ** -0.5)
    row = lax.broadcasted_iota(jnp.int32, (8, LANE), 0)
    lane = lax.broadcasted_iota(jnp.int32, (8, LANE), 1)
    sel = ((row == 0) & (lane < D_B)) | ((row == 1) & (lane >= D_B))
    qms = []
    for h in range(H_B):
        qm = jnp.where(sel, qrow[:, h * LANE:(h + 1) * LANE], 0.0).astype(BF16)
        qms.append(qm)
        sc = jnp.concatenate([_dot_nt(qm, kp[j][:, h, :].astype(BF16)) for j in range(PP)], axis=1)
        p, alpha, m_new, l_new = _online(sc, m_ref[h][:, :1], l_ref[h][:, :1])
        pv = jnp.dot(p[:, 0:PAGE].astype(BF16), vp[0][:, h, :].astype(BF16), preferred_element_type=F32)
        for j in range(1, PP):
            pv = pv + jnp.dot(p[:, j * PAGE:(j + 1) * PAGE].astype(BF16), vp[j][:, h, :].astype(BF16),
                              preferred_element_type=F32)
        m_ref[h] = jnp.broadcast_to(m_new, (8, LANE))
        l_ref[h] = jnp.broadcast_to(l_new, (8, LANE))
        acc_ref[h] = acc_ref[h] * alpha + pv

    @pl.when(s == NSTEP - 1)
    def _():
        lam = _lam(lq1, lk1, lq2, lk2, lam_init)
        for h in range(H_B):
            knh = kn_ref[:, h * LANE:(h + 1) * LANE].astype(BF16).astype(F32)
            vnh = vn_ref[:, h * LANE:(h + 1) * LANE]
            s_new = jnp.sum(qms[h].astype(F32) * knh, axis=-1, keepdims=True)
            m_old = m_ref[h][:, :1]
            m_f = jnp.maximum(m_old, s_new)
            alpha = jnp.exp(m_old - m_f)
            pn = jnp.exp(s_new - m_f)
            l_f = l_ref[h][:, :1] * alpha + pn
            o = (acc_ref[h] * alpha + pn * vnh) / l_f
            od = o[0:1, :] - lam * o[1:2, :]
            od = od * lax.rsqrt(jnp.mean(od * od, axis=-1, keepdims=True) + SUBLN_EPS) * sg_ref[...]
            gate = g_ref[:, h * LANE:(h + 1) * LANE]
            o_ref[:, h * LANE:(h + 1) * LANE] = od * (1.0 - lam_init) * _silu(gate)


def _diff_decode(page_flat, q, kn, vn, g, cache_k, cache_v, layer, subln_g, lq1, lk1, lq2, lk2,
                 lam_init):
    n = q.shape[0]
    full = pl.BlockSpec((None, 1, W_B), lambda b, s, pt: (b, 0, 0))
    small = pl.BlockSpec((1, D_B), lambda b, s, pt: (0, 0))
    seq = lambda x: x.reshape(n, 1, W_B)

    def page_spec(j):
        return pl.BlockSpec((None, None, PAGE, H_B, 2 * D_B),
                            lambda b, s, pt: (layer, pt[b * N_PAGES + s * PP + j], 0, 0, 0))

    grid_spec = pltpu.PrefetchScalarGridSpec(
        num_scalar_prefetch=1,
        grid=(n, NSTEP),
        in_specs=[full, full, full, full, pl.BlockSpec((1, LANE), lambda b, s, pt: (0, 0)),
                  small, small, small, small]
                 + [page_spec(j) for j in range(PP)] + [page_spec(j) for j in range(PP)],
        out_specs=full,
        scratch_shapes=[pltpu.VMEM((H_B, 8, LANE), F32)] * 3,
    )
    return pl.pallas_call(
        functools.partial(_diff_dec_kernel, lam_init=lam_init),
        grid_spec=grid_spec,
        out_shape=jax.ShapeDtypeStruct((n, 1, W_B), F32),
        compiler_params=_cparams(("arbitrary", "arbitrary")),
        name="diff_decode",
    )(page_flat, seq(q), seq(kn), seq(vn), seq(g), subln_g.reshape(1, -1), lq1.reshape(1, -1),
      lk1.reshape(1, -1), lq2.reshape(1, -1), lk2.reshape(1, -1),
      *([cache_k] * PP), *([cache_v] * PP)).reshape(n, W_B)


def _fox_dec_kernel(pt_ref, q_ref, kn_ref, vn_ref, g_ref, lfn_ref, tri_ref, *rest):
    kp = rest[:PP]
    vp = rest[PP:2 * PP]
    lfp = rest[2 * PP:3 * PP]
    o_ref, m_ref, l_ref, c_ref, acc_ref = rest[3 * PP:]
    b = pl.program_id(0)
    s = pl.program_id(1)

    @pl.when(s == 0)
    def _():
        m_ref[...] = jnp.full_like(m_ref, NEG)
        l_ref[...] = jnp.zeros_like(l_ref)
        c_ref[...] = jnp.zeros_like(c_ref)
        acc_ref[...] = jnp.zeros_like(acc_ref)

    qrow = q_ref[...] * (D_C ** -0.5)
    row = lax.broadcasted_iota(jnp.int32, (8, D_C), 0)
    qms = [jnp.where(row == (h % 8), qrow[:, h * D_C:(h + 1) * D_C], 0.0).astype(BF16)
           for h in range(H_C)]
    tri = tri_ref[...]
    carry = c_ref[...][:, :1]
    tiles = []
    for j in range(PP):
        t0 = _dot_nt(qms[0], kp[j][:, 0, :].astype(BF16))
        for h in range(1, 8):
            t0 = t0 + _dot_nt(qms[h], kp[j][:, h, :].astype(BF16))
        t1 = _dot_nt(qms[8], kp[j][:, 8, :].astype(BF16))
        for h in range(9, H_C):
            t1 = t1 + _dot_nt(qms[h], kp[j][:, h, :].astype(BF16))
        lf = lfp[j][...]
        cum = jnp.dot(lf, tri, precision=HI, preferred_element_type=F32) + carry
        carry = carry + jnp.sum(lf, axis=-1, keepdims=True)
        tiles.append(jnp.concatenate([t0, t1], axis=0) - cum)
    sc = jnp.concatenate(tiles, axis=1)
    p, alpha, m_new, l_new = _online(sc, m_ref[...][:, :1], l_ref[...][:, :1])
    m_ref[...] = jnp.broadcast_to(m_new, (16, LANE))
    l_ref[...] = jnp.broadcast_to(l_new, (16, LANE))
    c_ref[...] = jnp.broadcast_to(carry, (16, LANE))
    for h in range(H_C):
        g8 = 8 * (h // 8)
        pv = jnp.dot(p[g8:g8 + 8, 0:PAGE].astype(BF16), vp[0][:, h, :].astype(BF16),
                     preferred_element_type=F32)
        for j in range(1, PP):
            pv = pv + jnp.dot(p[g8:g8 + 8, j * PAGE:(j + 1) * PAGE].astype(BF16),
                              vp[j][:, h, :].astype(BF16), preferred_element_type=F32)
        acc_ref[h] = acc_ref[h] * alpha[g8:g8 + 8, :] + pv

    @pl.when(s == NSTEP - 1)
    def _():
        sn = []
        for h in range(H_C):
            knh = kn_ref[:, h * D_C:(h + 1) * D_C].astype(BF16).astype(F32)
            sn.append(jnp.sum(qms[h].astype(F32) * knh, axis=-1, keepdims=True))
        t0 = sn[0]
        for h in range(1, 8):
            t0 = t0 + sn[h]
        t1 = sn[8]
        for h in range(9, H_C):
            t1 = t1 + sn[h]
        cum_new = carry + lfn_ref[...][:, :1]
        s_new = jnp.concatenate([t0, t1], axis=0) - cum_new
        m_f = jnp.maximum(m_new, s_new)
        al = jnp.exp(m_new - m_f)
        pn = jnp.exp(s_new - m_f)
        l_f = l_new * al + pn
        outs = []
        for h in range(H_C):
            g8 = 8 * (h // 8)
            vnh = vn_ref[:, h * D_C:(h + 1) * D_C]
            o = (acc_ref[h] * al[g8:g8 + 8, :] + pn[g8:g8 + 8, :] * vnh) / l_f[g8:g8 + 8, :]
            outs.append(o[h % 8:h % 8 + 1, :])
        o_all = jnp.concatenate(outs, axis=1)
        o_ref[...] = o_all * _silu(g_ref[...])


def _fox_decode(page_flat, q, kn, vn, g, lfn_rep, cache_k, cache_v, cache_lft, layer):
    n = q.shape[0]
    full = pl.BlockSpec((None, 1, W_C), lambda b, s, pt: (b, 0, 0))
    seq = lambda x: x.reshape(n, 1, W_C)
    tri = (jnp.arange(PAGE)[:, None] <= jnp.arange(PAGE)[None, :]).astype(F32)

    def page_spec(j):
        return pl.BlockSpec((None, None, PAGE, H_C, D_C),
                            lambda b, s, pt: (layer, pt[b * N_PAGES + s * PP + j], 0, 0, 0))

    def lf_spec(j):
        return pl.BlockSpec((None, None, 16, PAGE),
                            lambda b, s, pt: (layer, pt[b * N_PAGES + s * PP + j], 0, 0))

    grid_spec = pltpu.PrefetchScalarGridSpec(
        num_scalar_prefetch=1,
        grid=(n, NSTEP),
        in_specs=[full, full, full, full,
                  pl.BlockSpec((None, 16, LANE), lambda b, s, pt: (b, 0, 0)),
                  pl.BlockSpec((PAGE, PAGE), lambda b, s, pt: (0, 0))]
                 + [page_spec(j) for j in range(PP)] + [page_spec(j) for j in range(PP)]
                 + [lf_spec(j) for j in range(PP)],
        out_specs=full,
        scratch_shapes=[pltpu.VMEM((16, LANE), F32)] * 3 + [pltpu.VMEM((H_C, 8, D_C), F32)],
    )
    return pl.pallas_call(
        _fox_dec_kernel,
        grid_spec=grid_spec,
        out_shape=jax.ShapeDtypeStruct((n, 1, W_C), F32),
        compiler_params=_cparams(("arbitrary", "arbitrary")),
        name="fox_decode",
    )(page_flat, seq(q), seq(kn), seq(vn), seq(g), lfn_rep, tri, *([cache_k] * PP),
      *([cache_v] * PP), *([cache_lft] * PP)).reshape(n, W_C)


def _merge_kernel(ga_ref, gb_ref, gc_ref, ag_ref, y_ref, bonus_ref, gng_ref, gnb_ref, bd_ref,
                  yb_ref, yc_ref, w_ref, o_ref):
    y = y_ref[...]
    bd = bd_ref[...]
    mean = jnp.dot(y, bd, precision=HI, preferred_element_type=F32) * (1.0 / N_A)
    d = y - mean
    var = jnp.dot(d * d, bd, precision=HI, preferred_element_type=F32) * (1.0 / N_A)
    yn = d * lax.rsqrt(var + GN_EPS) * gng_ref[...] + gnb_ref[...] + bonus_ref[...]
    ya = yn * _silu(ag_ref[...])
    pa = jnp.dot(ya.astype(BF16), w_ref[0:W_A, :], preferred_element_type=F32)
    pb = jnp.dot(yb_ref[...].astype(BF16), w_ref[W_A:W_A + W_B, :], preferred_element_type=F32)
    pc = jnp.dot(yc_ref[...].astype(BF16), w_ref[W_A + W_B:, :], preferred_element_type=F32)
    merged = _sigmoid(ga_ref[...]) * pa + _sigmoid(gb_ref[...]) * pb + _sigmoid(gc_ref[...]) * pc
    o_ref[...] = merged.astype(BF16)


def _merge(proj, y_raw, bonus, gn_g, gn_b, bd, yb, yc, w_br, tm):
    m = proj.shape[0]
    gate = lambda c: pl.BlockSpec((tm, D_MODEL), lambda i: (i, c))
    tok = lambda n: pl.BlockSpec((tm, n), lambda i: (i, 0))
    row = pl.BlockSpec((1, W_A), lambda i: (0, 0))
    return pl.pallas_call(
        _merge_kernel,
        grid=(m // tm,),
        in_specs=[gate(0), gate(1), gate(2),
                  pl.BlockSpec((tm, W_A), lambda i: (i, C_AGATE // W_A)),
                  tok(W_A), tok(W_A), row, row, pl.BlockSpec((W_A, W_A), lambda i: (0, 0)),
                  tok(W_B), tok(W_C), pl.BlockSpec((D_MODEL, D_MODEL), lambda i: (0, 0))],
        out_specs=tok(D_MODEL),
        out_shape=jax.ShapeDtypeStruct((m, D_MODEL), BF16),
        compiler_params=_cparams(("parallel",)),
        name="merge",
    )(proj, proj, proj, proj, y_raw, bonus, gn_g.reshape(1, -1), gn_b.reshape(1, -1), bd, yb, yc, w_br)


def _outproj_kernel(x_ref, m_ref, w_ref, o_ref):
    o_ref[...] = x_ref[...] + jnp.dot(m_ref[...], w_ref[...], preferred_element_type=F32)


def _outproj(x2d, merged, w_out, tm):
    m = x2d.shape[0]
    tok = pl.BlockSpec((tm, D_MODEL), lambda i: (i, 0))
    return pl.pallas_call(
        _outproj_kernel,
        grid=(m // tm,),
        in_specs=[tok, tok, pl.BlockSpec((D_MODEL, D_MODEL), lambda i: (0, 0))],
        out_specs=tok,
        out_shape=jax.ShapeDtypeStruct((m, D_MODEL), F32),
        compiler_params=_cparams(("parallel",)),
        name="outproj",
    )(x2d, merged, w_out)


def _rmsnorm_kernel(x_ref, g_ref, o_ref):
    x = x_ref[...]
    o_ref[...] = x * lax.rsqrt(jnp.mean(x * x, axis=-1, keepdims=True) + NORM_EPS) * g_ref[...]


def _rmsnorm(x2d, g, tm):
    m = x2d.shape[0]
    tok = pl.BlockSpec((tm, D_MODEL), lambda i: (i, 0))
    return pl.pallas_call(
        _rmsnorm_kernel,
        grid=(m // tm,),
        in_specs=[tok, pl.BlockSpec((1, D_MODEL), lambda i: (0, 0))],
        out_specs=tok,
        out_shape=jax.ShapeDtypeStruct((m, D_MODEL), F32),
        compiler_params=_cparams(("parallel",)),
        name="final_norm",
    )(x2d, g.reshape(1, -1))


def _permute_w_in(w):
    pad = jnp.zeros((D_MODEL, LANE - H_C), w.dtype)
    return jnp.concatenate([w[:, 8332:14476], w[:, 1664:2176], w[:, 0:1664], w[:, 2176:7552],
                            w[:, 7564:8332], w[:, 7552:7564], pad], axis=1).astype(BF16)


def kernel(x_prompt, x_sample, cache_diff_k, cache_diff_v, cache_fox_k, cache_fox_v, cache_fox_logf,
           state_rwkv_wkv, state_rwkv_shift, page_table, norm_g, w_in, rwkv_mu, rwkv_w0, rwkv_w2,
           rwkv_a0, rwkv_a2, rwkv_k_k, rwkv_k_a, rwkv_r_k, rwkv_gn_g, rwkv_gn_b, diff_lq1, diff_lk1,
           diff_lq2, diff_lk2, diff_subln_g, fox_b_f, w_branch, w_out, final_g):
    bsz, t, _ = x_prompt.shape
    n = x_sample.shape[0]
    mp = bsz * t
    hp = x_prompt.reshape(mp, D_MODEL)
    hs = x_sample.reshape(n, D_MODEL)
    page_flat = page_table.reshape(-1).astype(jnp.int32)
    bd = jnp.kron(jnp.eye(H_A, dtype=F32), jnp.ones((N_A, N_A), F32))
    lft_all = jnp.pad(jnp.swapaxes(cache_fox_logf, 2, 3), ((0, 0), (0, 0), (0, 16 - H_C), (0, 0)))
    zero_lora = jnp.zeros((64, W_A), BF16)
    outs_p, outs_s = [], []
    for l in range(DEPTH):
        lam_init = 0.8 - 0.6 * math.exp(-0.3 * l)
        w_p = _permute_w_in(w_in[l])
        w2p = jnp.concatenate([rwkv_w2[l].astype(BF16), zero_lora], axis=0)
        a2p = jnp.concatenate([zero_lora, rwkv_a2[l].astype(BF16)], axis=0)
        bf_pad = jnp.pad(fox_b_f[l], (0, LANE - H_C)).reshape(1, LANE)
        w_br = w_branch[l].astype(BF16)
        w_o = w_out[l].astype(BF16)
        rk = rwkv_r_k[l].reshape(-1)

        proj = _inproj(hp, norm_g[l], w_p, tm=1024)
        u = proj[:, C_U:C_U + A_SHIFT_W].reshape(bsz, t, A_SHIFT_W)
        prev = jnp.concatenate([jnp.zeros((bsz, 1, A_SHIFT_W), F32), u[:, :-1]], axis=1)
        r, dec, k2, v, kk, bb, bonus = _rwkv_prep(
            proj, prev.reshape(mp, A_SHIFT_W), rwkv_mu[l], rwkv_w0[l], rwkv_a0[l], rwkv_k_k[l],
            rwkv_k_a[l], rk, w2p, a2p, bd, tm=512)
        y_t, s_fin = _rwkv_scan(
            _to_scan_tiles(dec, bsz, t), _to_scan_tiles(kk, bsz, t), _to_scan_tiles(bb, bsz, t),
            _to_scan_tiles(k2, bsz, t), _to_scan_tiles(r, bsz, t), _to_scan_v(v, bsz, t),
            jnp.zeros((V_HI, N_A, LANE), F32))
        y_raw = _from_scan_y(y_t, bsz, t)
        yb = _diff_attn(proj, bsz, t, diff_subln_g[l], diff_lq1[l], diff_lk1[l], diff_lq2[l],
                        diff_lk2[l], lam_init)
        lf, cum, cumt = _foxcum(proj, bsz, t, bf_pad, tm=256)
        yc = _fox_attn(proj, cum, cumt, bsz, t)
        merged = _merge(proj, y_raw, bonus, rwkv_gn_g[l], rwkv_gn_b[l], bd, yb, yc, w_br, tm=256)
        hp = _outproj(hp, merged, w_o, tm=512)
        outs_p.append((
            proj[:, C_BK:C_BK + W_B].reshape(bsz, t, H_B, 2 * D_B),
            proj[:, C_BV:C_BV + W_B].reshape(bsz, t, H_B, 2 * D_B),
            proj[:, C_CK:C_CK + W_C].reshape(bsz, t, H_C, D_C),
            proj[:, C_CV:C_CV + W_C].reshape(bsz, t, H_C, D_C),
            lf[:, :H_C].reshape(bsz, t, H_C),
            _from_scan_state(s_fin, bsz),
            u[:, -1]))

        projs = _inproj(hs, norm_g[l], w_p, tm=n)
        us = projs[:, C_U:C_U + A_SHIFT_W]
        r, dec, k2, v, kk, bb, bonus = _rwkv_prep(
            projs, state_rwkv_shift[l], rwkv_mu[l], rwkv_w0[l], rwkv_a0[l], rwkv_k_k[l],
            rwkv_k_a[l], rk, w2p, a2p, bd, tm=n)
        s_new, y_col = _rwkv_step(state_rwkv_wkv, l, dec, kk, bb, k2, r, v)
        y_raw = y_col.reshape(n, W_A)
        lfs, _, _ = _foxcum(projs, 1, n, bf_pad, tm=n)
        q_b = projs[:, C_BQ:C_BQ + W_B]
        k_b = projs[:, C_BK:C_BK + W_B]
        v_b = projs[:, C_BV:C_BV + W_B]
        g_b = projs[:, C_BG:C_BG + W_B]
        yb = _diff_decode(page_flat, q_b, k_b, v_b, g_b, cache_diff_k, cache_diff_v, l,
                          diff_subln_g[l], diff_lq1[l], diff_lk1[l], diff_lq2[l], diff_lk2[l], lam_init)
        q_c = projs[:, C_CQ:C_CQ + W_C]
        k_c = projs[:, C_CK:C_CK + W_C]
        v_c = projs[:, C_CV:C_CV + W_C]
        g_c = projs[:, C_CG:C_CG + W_C]
        lfn_rep = jnp.broadcast_to(lfs[:, :16, None], (n, 16, LANE))
        yc = _fox_decode(page_flat, q_c, k_c, v_c, g_c, lfn_rep, cache_fox_k, cache_fox_v, lft_all, l)
        merged = _merge(projs, y_raw, bonus, rwkv_gn_g[l], rwkv_gn_b[l], bd, yb, yc, w_br, tm=n)
        hs = _outproj(hs, merged, w_o, tm=n)
        outs_s.append((
            k_b.reshape(n, 1, H_B, 2 * D_B), v_b.reshape(n, 1, H_B, 2 * D_B),
            k_c.reshape(n, 1, H_C, D_C), v_c.reshape(n, 1, H_C, D_C),
            lfs[:, :H_C].reshape(n, 1, H_C), s_new, us))

    y_prompt = _rmsnorm(hp, final_g, tm=512).reshape(bsz, t, D_MODEL)
    y_sample = _rmsnorm(hs, final_g, tm=n).reshape(n, 1, D_MODEL)
    st_p = [jnp.stack([o[i] for o in outs_p]) for i in range(7)]
    st_s = [jnp.stack([o[i] for o in outs_s]) for i in range(7)]
    return (y_prompt, y_sample, st_p[0], st_s[0], st_p[1], st_s[1], st_p[2], st_s[2], st_p[3], st_s[3],
            st_p[4], st_s[4], st_p[5], st_s[5], st_p[6], st_s[6])
```

```python
import functools
import math

import jax
import jax.numpy as jnp
from jax import lax
from jax.experimental import pallas as pl
from jax.experimental.pallas import tpu as pltpu

F32 = jnp.float32
BF16 = jnp.bfloat16
HI = lax.Precision.HIGHEST

D_MODEL = 2048
DEPTH = 2
PAGE = 128
H_A, N_A = 8, 64
W_A = H_A * N_A
A_SHIFT_W = 3 * W_A + 128
H_B, D_B = 6, 64
W_B = H_B * 2 * D_B
H_C, D_C = 12, 64
W_C = H_C * D_C
NORM_EPS = 1e-6
SUBLN_EPS = 1e-5
GN_EPS = 64e-5
LANE = 128

C_GATES = 0
C_AGATE = 6144
C_U = 6656
C_BQ, C_BK, C_BV, C_BG = 8320, 9088, 9856, 10624
C_CQ, C_CK, C_CV, C_CG = 11392, 12160, 12928, 13696
C_CF = 14464
NP = 14592
TN_IN = 768

NEG = -1e30


def _cparams(sem, vmem_mb=48):
    return pltpu.CompilerParams(dimension_semantics=sem, vmem_limit_bytes=vmem_mb * 1024 * 1024)


def _softplus(z):
    return jnp.maximum(z, 0.0) + jnp.log1p(jnp.exp(-jnp.abs(z)))


def _sigmoid(z):
    return jax.nn.sigmoid(z)


def _silu(z):
    return z * jax.nn.sigmoid(z)


def _dot_nt(a, b):
    return lax.dot_general(a, b, (((1,), (1,)), ((), ())), preferred_element_type=F32)


def _inproj_kernel(x_ref, g_ref, w_ref, o_ref, h_ref):
    @pl.when(pl.program_id(1) == 0)
    def _():
        x = x_ref[...]
        ms = jnp.mean(x * x, axis=-1, keepdims=True)
        h_ref[...] = (x * lax.rsqrt(ms + NORM_EPS) * g_ref[...]).astype(BF16)

    o_ref[...] = _dot_nt(h_ref[...], w_ref[...])


def _inproj(x2d, g, w_p, tm):
    m = x2d.shape[0]
    return pl.pallas_call(
        _inproj_kernel,
        grid=(m // tm, NP // TN_IN),
        in_specs=[pl.BlockSpec((tm, D_MODEL), lambda i, j: (i, 0)),
                  pl.BlockSpec((1, D_MODEL), lambda i, j: (0, 0)),
                  pl.BlockSpec((TN_IN, D_MODEL), lambda i, j: (j, 0))],
        out_specs=pl.BlockSpec((tm, TN_IN), lambda i, j: (i, j)),
        out_shape=jax.ShapeDtypeStruct((m, NP), F32),
        scratch_shapes=[pltpu.VMEM((tm, D_MODEL), BF16)],
        compiler_params=_cparams(("parallel", "arbitrary")),
        name="inproj",
    )(x2d, g.reshape(1, D_MODEL), w_p)


def _rwkv_prep_kernel(u_ref, prev_ref, mu_ref, w0_ref, a0_ref, kk_ref, ka_ref, rk_ref,
                      w2_ref, a2_ref, bd_ref,
                      r_o, w_o, k_o, v_o, kk_o, b_o, bonus_o):
    u = u_ref[...]
    m = u + mu_ref[...] * (prev_ref[...] - u)
    r = m[:, 0:W_A]
    k = m[:, W_A:2 * W_A]
    v = m[:, 2 * W_A:3 * W_A]
    lo = m[:, 3 * W_A:3 * W_A + 128]
    zw = w0_ref[...] + jnp.dot(jnp.tanh(lo).astype(BF16), w2_ref[...], preferred_element_type=F32)
    w_log = -_softplus(-zw) - 0.5
    a = _sigmoid(a0_ref[...] + jnp.dot(lo.astype(BF16), a2_ref[...], preferred_element_type=F32))
    decay = jnp.exp(-jnp.exp(w_log))
    bd = bd_ref[...]
    kk = k * kk_ref[...]
    ss = jnp.dot(kk * kk, bd, precision=HI, preferred_element_type=F32)
    kk = kk * lax.rsqrt(jnp.maximum(ss, 1e-24))
    k2 = k * (1.0 + (a - 1.0) * ka_ref[...])
    bonus = jnp.dot(r * k2 * rk_ref[...], bd, precision=HI, preferred_element_type=F32) * v
    r_o[...] = r
    w_o[...] = decay
    k_o[...] = k2
    v_o[...] = v
    kk_o[...] = kk
    b_o[...] = kk * a
    bonus_o[...] = bonus


def _rwkv_prep(proj, prev, mu, w0, a0, k_k, k_a, r_k, w2p, a2p, bd, tm):
    m = proj.shape[0]
    row = lambda n: pl.BlockSpec((1, n), lambda i: (0, 0))
    full = lambda a: pl.BlockSpec(a.shape, lambda i: (0, 0))
    tok = pl.BlockSpec((tm, W_A), lambda i: (i, 0))
    return pl.pallas_call(
        _rwkv_prep_kernel,
        grid=(m // tm,),
        in_specs=[pl.BlockSpec((tm, A_SHIFT_W), lambda i: (i, C_U // A_SHIFT_W)),
                  pl.BlockSpec((tm, A_SHIFT_W), lambda i: (i, 0)),
                  row(A_SHIFT_W), row(W_A), row(W_A), row(W_A), row(W_A), row(W_A),
                  full(w2p), full(a2p), full(bd)],
        out_specs=[tok] * 7,
        out_shape=[jax.ShapeDtypeStruct((m, W_A), F32)] * 7,
        compiler_params=_cparams(("parallel",)),
        name="rwkv_prep",
    )(proj, prev, mu.reshape(1, -1), w0.reshape(1, -1), a0.reshape(1, -1), k_k.reshape(1, -1),
      k_a.reshape(1, -1), r_k.reshape(1, -1), w2p, a2p, bd)


V_HI = 16
SCAN_TC = 16


def _rwkv_scan_kernel(w_ref, kk_ref, b_ref, k_ref, r_ref, v_ref, s0_ref, y_ref, s_ref):
    @pl.when(pl.program_id(0) == 0)
    def _():
        s_ref[...] = s0_ref[...]

    def step(t, carry):
        w = w_ref[t]
        kk = kk_ref[t]
        b = b_ref[t]
        kv = k_ref[t]
        r = r_ref[t]
        vt = v_ref[t]
        rows = []
        for vh in range(V_HI):
            s = s_ref[vh]
            sk = jnp.sum(s * kk, axis=0, keepdims=True)
            s2 = s * w - sk * b + vt[vh:vh + 1, :] * kv
            s_ref[vh] = s2
            rows.append(jnp.sum(s2 * r, axis=0, keepdims=True))
        y_ref[t] = jnp.concatenate(rows, axis=0)
        return carry

    lax.fori_loop(0, SCAN_TC, step, 0)


def _rwkv_scan(w_t, kk_t, b_t, k_t, r_t, v_t, s0):
    t = w_t.shape[0]
    tile = pl.BlockSpec((SCAN_TC, N_A, LANE), lambda i: (i, 0, 0))
    vspec = pl.BlockSpec((SCAN_TC, V_HI, LANE), lambda i: (i, 0, 0))
    sspec = pl.BlockSpec((V_HI, N_A, LANE), lambda i: (0, 0, 0))
    return pl.pallas_call(
        _rwkv_scan_kernel,
        grid=(t // SCAN_TC,),
        in_specs=[tile] * 5 + [vspec, sspec],
        out_specs=[vspec, sspec],
        out_shape=[jax.ShapeDtypeStruct((t, V_HI, LANE), F32),
                   jax.ShapeDtypeStruct((V_HI, N_A, LANE), F32)],
        compiler_params=_cparams(("arbitrary",)),
        name="rwkv_scan",
    )(w_t, kk_t, b_t, k_t, r_t, v_t, s0)


def _to_scan_tiles(x, bsz, t):
    x = x.reshape(bsz, t, H_A, N_A).transpose(1, 3, 0, 2).reshape(t, N_A, 1, bsz * H_A)
    return jnp.broadcast_to(x, (t, N_A, LANE // (bsz * H_A), bsz * H_A)).reshape(t, N_A, LANE)


def _to_scan_v(v, bsz, t):
    v = v.reshape(bsz, t, H_A, V_HI, N_A // V_HI).transpose(1, 3, 4, 0, 2)
    return v.reshape(t, V_HI, LANE)


def _from_scan_y(y, bsz, t):
    y = y.reshape(t, V_HI, N_A // V_HI, bsz, H_A).transpose(3, 0, 4, 1, 2)
    return y.reshape(bsz * t, W_A)


def _from_scan_state(s, bsz):
    s = s.reshape(V_HI, N_A, N_A // V_HI, bsz, H_A).transpose(3, 4, 0, 2, 1)
    return s.reshape(bsz, H_A, N_A, N_A)


def _rwkv_step_kernel(s_ref, w_ref, kk_ref, b_ref, k_ref, r_ref, v_ref, so_ref, y_ref):
    w = w_ref[...]
    kk = kk_ref[...]
    b = b_ref[...]
    kv = k_ref[...]
    r = r_ref[...]

    def body(v, carry):
        s = s_ref[v]
        sk = jnp.sum(s * kk, axis=0, keepdims=True)
        s2 = s * w - sk * b + v_ref[pl.ds(v, 1), :] * kv
        so_ref[v] = s2
        y_ref[pl.ds(v, 1), :] = jnp.sum(s2 * r, axis=0, keepdims=True)
        return carry

    lax.fori_loop(0, N_A, body, 0, unroll=4)


def _rwkv_step(state_t, layer, w, kk, b, k2, r, v):
    n = w.shape[0]
    tile = pl.BlockSpec((N_A, n), lambda h: (h, 0))
    return pl.pallas_call(
        _rwkv_step_kernel,
        grid=(H_A,),
        in_specs=[pl.BlockSpec((None, None, N_A, N_A, n), lambda h: (layer, h, 0, 0, 0)),
                  tile, tile, tile, tile, tile, tile],
        out_specs=[pl.BlockSpec((None, N_A, N_A, n), lambda h: (h, 0, 0, 0)), tile],
        out_shape=[jax.ShapeDtypeStruct((H_A, N_A, N_A, n), F32),
                   jax.ShapeDtypeStruct((W_A, n), F32)],
        compiler_params=_cparams(("parallel",)),
        name="rwkv_step",
    )(state_t, w.T, kk.T, b.T, k2.T, r.T, v.T)


def _lam(lq1, lk1, lq2, lk2, lam_init):
    return (jnp.exp(jnp.sum(lq1[...] * lk1[...], axis=-1, keepdims=True))
            - jnp.exp(jnp.sum(lq2[...] * lk2[...], axis=-1, keepdims=True)) + lam_init)


def _online(s, m, l):
    m_new = jnp.maximum(m, jnp.max(s, axis=-1, keepdims=True))
    alpha = jnp.exp(m - m_new)
    p = jnp.exp(s - m_new)
    return p, alpha, m_new, l * alpha + jnp.sum(p, axis=-1, keepdims=True)


def _diff_attn_kernel(q_ref, k_ref, v_ref, g_ref, sg_ref, lq1, lk1, lq2, lk2, o_ref, ko_ref, vo_ref,
                      *, lam_init, tq, tk):
    qi = pl.program_id(2)

    @pl.when(qi == 0)
    def _():
        ko_ref[...] = k_ref[...]
        vo_ref[...] = v_ref[...]

    lane = lax.broadcasted_iota(jnp.int32, (1, LANE), 1)
    q = q_ref[...] * (D_B ** -0.5)
    q1 = jnp.where(lane < D_B, q, 0.0).astype(BF16)
    q2 = jnp.where(lane >= D_B, q, 0.0).astype(BF16)
    rows = qi * tq + lax.broadcasted_iota(jnp.int32, (tq, tk), 0)
    cols0 = lax.broadcasted_iota(jnp.int32, (tq, tk), 1)

    def body(j, carry):
        m1, l1, a1, m2, l2, a2 = carry
        off = pl.multiple_of(j * tk, tk)
        kb = k_ref[pl.ds(off, tk), :].astype(BF16)
        vb = v_ref[pl.ds(off, tk), :].astype(BF16)
        mask = (cols0 + off) <= rows
        s1 = jnp.where(mask, _dot_nt(q1, kb), NEG)
        s2 = jnp.where(mask, _dot_nt(q2, kb), NEG)
        p1, al1, m1, l1 = _online(s1, m1, l1)
        p2, al2, m2, l2 = _online(s2, m2, l2)
        a1 = a1 * al1 + jnp.dot(p1.astype(BF16), vb, preferred_element_type=F32)
        a2 = a2 * al2 + jnp.dot(p2.astype(BF16), vb, preferred_element_type=F32)
        return m1, l1, a1, m2, l2, a2

    mi = jnp.full((tq, 1), NEG, F32)
    zi = jnp.zeros((tq, 1), F32)
    ai = jnp.zeros((tq, LANE), F32)
    nkv = (qi * tq + tq + tk - 1) // tk
    m1, l1, a1, m2, l2, a2 = lax.fori_loop(0, nkv, body, (mi, zi, ai, mi, zi, ai))
    lam = _lam(lq1, lk1, lq2, lk2, lam_init)
    o = a1 / l1 - lam * (a2 / l2)
    o = o * lax.rsqrt(jnp.mean(o * o, axis=-1, keepdims=True) + SUBLN_EPS) * sg_ref[...]
    o_ref[...] = o * (1.0 - lam_init) * _silu(g_ref[...])


def _diff_attn(proj, bsz, t, subln_g, lq1, lk1, lq2, lk2, lam_init, tq=256, tk=256):
    nq = t // tq
    small = pl.BlockSpec((1, D_B), lambda b, h, i: (0, 0))
    return pl.pallas_call(
        functools.partial(_diff_attn_kernel, lam_init=lam_init, tq=tq, tk=tk),
        grid=(bsz, H_B, nq),
        in_specs=[pl.BlockSpec((tq, LANE), lambda b, h, i: (b * nq + i, C_BQ // LANE + h)),
                  pl.BlockSpec((t, LANE), lambda b, h, i: (b, C_BK // LANE + h)),
                  pl.BlockSpec((t, LANE), lambda b, h, i: (b, C_BV // LANE + h)),
                  pl.BlockSpec((tq, LANE), lambda b, h, i: (b * nq + i, C_BG // LANE + h)),
                  pl.BlockSpec((1, LANE), lambda b, h, i: (0, 0)),
                  small, small, small, small],
        out_specs=[pl.BlockSpec((tq, LANE), lambda b, h, i: (b * nq + i, h)),
                   pl.BlockSpec((None, None, t, LANE), lambda b, h, i: (b, h, 0, 0)),
                   pl.BlockSpec((None, None, t, LANE), lambda b, h, i: (b, h, 0, 0))],
        out_shape=[jax.ShapeDtypeStruct((bsz * t, W_B), F32),
                   jax.ShapeDtypeStruct((bsz, H_B, t, LANE), F32),
                   jax.ShapeDtypeStruct((bsz, H_B, t, LANE), F32)],
        compiler_params=_cparams(("parallel", "parallel", "arbitrary")),
        name="diff_attn",
    )(proj, proj, proj, proj, subln_g.reshape(1, -1), lq1.reshape(1, -1), lk1.reshape(1, -1),
      lq2.reshape(1, -1), lk2.reshape(1, -1))


def _foxcum_kernel(cf_ref, bf_ref, tri_ref, lf_ref, cum_ref, cumt_ref, lft_ref, carry_ref, *, tm):
    @pl.when(pl.program_id(1) == 0)
    def _():
        carry_ref[...] = jnp.zeros_like(carry_ref)

    lf = -_softplus(-(cf_ref[...] + bf_ref[...]))
    lf_ref[...] = lf
    lft_ref[...] = lf.T[:16, :]
    c = jnp.dot(tri_ref[...], lf, precision=HI, preferred_element_type=F32) + carry_ref[...]
    cum_ref[...] = c
    carry_ref[...] = c[tm - 1:tm, :]
    cumt_ref[...] = c.T[:16, :]


def _foxcum(proj, bsz, t, bf_pad, tm):
    nt = t // tm
    tri = (jnp.arange(tm)[:, None] >= jnp.arange(tm)[None, :]).astype(F32)
    tok = pl.BlockSpec((tm, LANE), lambda b, i: (b * nt + i, 0))
    return pl.pallas_call(
        functools.partial(_foxcum_kernel, tm=tm),
        grid=(bsz, nt),
        in_specs=[pl.BlockSpec((tm, LANE), lambda b, i: (b * nt + i, C_CF // LANE)),
                  pl.BlockSpec((1, LANE), lambda b, i: (0, 0)),
                  pl.BlockSpec((tm, tm), lambda b, i: (0, 0))],
        out_specs=[tok, tok, pl.BlockSpec((None, 16, tm), lambda b, i: (b, 0, i)),
                   pl.BlockSpec((None, 16, tm), lambda b, i: (b, 0, i))],
        out_shape=[jax.ShapeDtypeStruct((bsz * t, LANE), F32),
                   jax.ShapeDtypeStruct((bsz * t, LANE), F32),
                   jax.ShapeDtypeStruct((bsz, 16, t), F32),
                   jax.ShapeDtypeStruct((bsz, 16, t), F32)],
        scratch_shapes=[pltpu.VMEM((1, LANE), F32)],
        compiler_params=_cparams(("parallel", "arbitrary")),
        name="fox_cum",
    )(proj, bf_pad, tri)


def _fox_attn_kernel(q_ref, k_ref, v_ref, g_ref, cum_ref, cumt_ref, o_ref, kto_ref, vto_ref, *, tq, tk):
    hp = pl.program_id(1)
    qi = pl.program_id(2)

    @pl.when(qi == 0)
    def _():
        kto_ref[...] = k_ref[...].T
        vto_ref[...] = v_ref[...].T

    lane = lax.broadcasted_iota(jnp.int32, (1, LANE), 1)
    q = q_ref[...] * (D_C ** -0.5)
    qa = jnp.where(lane < D_C, q, 0.0).astype(BF16)
    qb = jnp.where(lane >= D_C, q, 0.0).astype(BF16)
    cq = cum_ref[...]
    cqa = jnp.sum(jnp.where(lane == 2 * hp, cq, 0.0), axis=-1, keepdims=True)
    cqb = jnp.sum(jnp.where(lane == 2 * hp + 1, cq, 0.0), axis=-1, keepdims=True)
    rows = qi * tq + lax.broadcasted_iota(jnp.int32, (tq, tk), 0)
    cols0 = lax.broadcasted_iota(jnp.int32, (tq, tk), 1)

    def body(j, carry):
        ma, la, aa, mb, lb, ab = carry
        off = pl.multiple_of(j * tk, tk)
        kb = k_ref[pl.ds(off, tk), :].astype(BF16)
        vb = v_ref[pl.ds(off, tk), :].astype(BF16)
        cka = cumt_ref[pl.ds(2 * hp, 1), pl.ds(off, tk)]
        ckb = cumt_ref[pl.ds(2 * hp + 1, 1), pl.ds(off, tk)]
        mask = (cols0 + off) <= rows
        sa = jnp.where(mask, _dot_nt(qa, kb) + (cqa - cka), NEG)
        sb = jnp.where(mask, _dot_nt(qb, kb) + (cqb - ckb), NEG)
        pa, ala, ma, la = _online(sa, ma, la)
        pb, alb, mb, lb = _online(sb, mb, lb)
        aa = aa * ala + jnp.dot(pa.astype(BF16), vb, preferred_element_type=F32)
        ab = ab * alb + jnp.dot(pb.astype(BF16), vb, preferred_element_type=F32)
        return ma, la, aa, mb, lb, ab

    mi = jnp.full((tq, 1), NEG, F32)
    zi = jnp.zeros((tq, 1), F32)
    ai = jnp.zeros((tq, LANE), F32)
    nkv = (qi * tq + tq + tk - 1) // tk
    ma, la, aa, mb, lb, ab = lax.fori_loop(0, nkv, body, (mi, zi, ai, mi, zi, ai))
    o = jnp.where(lane < D_C, aa / la, ab / lb)
    o_ref[...] = o * _silu(g_ref[...])


def _fox_attn(proj, cum, cumt, bsz, t, tq=256, tk=256):
    nq = t // tq
    return pl.pallas_call(
        functools.partial(_fox_attn_kernel, tq=tq, tk=tk),
        grid=(bsz, H_C // 2, nq),
        in_specs=[pl.BlockSpec((tq, LANE), lambda b, h, i: (b * nq + i, C_CQ // LANE + h)),
                  pl.BlockSpec((t, LANE), lambda b, h, i: (b, C_CK // LANE + h)),
                  pl.BlockSpec((t, LANE), lambda b, h, i: (b, C_CV // LANE + h)),
                  pl.BlockSpec((tq, LANE), lambda b, h, i: (b * nq + i, C_CG // LANE + h)),
                  pl.BlockSpec((tq, LANE), lambda b, h, i: (b * nq + i, 0)),
                  pl.BlockSpec((None, 16, t), lambda b, h, i: (b, 0, 0))],
        out_specs=[pl.BlockSpec((tq, LANE), lambda b, h, i: (b * nq + i, h)),
                   pl.BlockSpec((None, None, LANE, t), lambda b, h, i: (b, h, 0, 0)),
                   pl.BlockSpec((None, None, LANE, t), lambda b, h, i: (b, h, 0, 0))],
        out_shape=[jax.ShapeDtypeStruct((bsz * t, W_C), F32),
                   jax.ShapeDtypeStruct((bsz, H_C // 2, LANE, t), F32),
                   jax.ShapeDtypeStruct((bsz, H_C // 2, LANE, t), F32)],
        compiler_params=_cparams(("parallel", "parallel", "arbitrary")),
        name="fox_attn",
    )(proj, proj, proj, proj, cum, cumt)


PP = 8
N_PAGES = 16
NSTEP = N_PAGES // PP


def _diff_dec_kernel(pt_ref, q_ref, kn_ref, vn_ref, g_ref, sg_ref, lq1, lk1, lq2, lk2, *rest,
                     lam_init):
    kp = rest[:PP]
    vp = rest[PP:2 * PP]
    o_ref, m_ref, l_ref, acc_ref = rest[2 * PP:]
    b = pl.program_id(0)
    s = pl.program_id(1)

    @pl.when(s == 0)
    def _():
        m_ref[...] = jnp.full_like(m_ref, NEG)
        l_ref[...] = jnp.zeros_like(l_ref)
        acc_ref[...] = jnp.zeros_like(acc_ref)

    qrow = q_ref[...] * (D_B ** -0.5)
    row = lax.broadcasted_iota(jnp.int32, (8, LANE), 0)
    lane = lax.broadcasted_iota(jnp.int32, (8, LANE), 1)
    sel = ((row == 0) & (lane < D_B)) | ((row == 1) & (lane >= D_B))
    qms = []
    for h in range(H_B):
        qm = jnp.where(sel, qrow[:, h * LANE:(h + 1) * LANE], 0.0).astype(BF16)
        qms.append(qm)
        sc = jnp.concatenate([_dot_nt(qm, kp[j][h].astype(BF16)) for j in range(PP)], axis=1)
        p, alpha, m_new, l_new = _online(sc, m_ref[h][:, :1], l_ref[h][:, :1])
        pv = jnp.dot(p[:, 0:PAGE].astype(BF16), vp[0][h].astype(BF16), preferred_element_type=F32)
        for j in range(1, PP):
            pv = pv + jnp.dot(p[:, j * PAGE:(j + 1) * PAGE].astype(BF16), vp[j][h].astype(BF16),
                              preferred_element_type=F32)
        m_ref[h] = jnp.broadcast_to(m_new, (8, LANE))
        l_ref[h] = jnp.broadcast_to(l_new, (8, LANE))
        acc_ref[h] = acc_ref[h] * alpha + pv

    @pl.when(s == NSTEP - 1)
    def _():
        lam = _lam(lq1, lk1, lq2, lk2, lam_init)
        for h in range(H_B):
            knh = kn_ref[:, h * LANE:(h + 1) * LANE].astype(BF16).astype(F32)
            vnh = vn_ref[:, h * LANE:(h + 1) * LANE]
            s_new = jnp.sum(qms[h].astype(F32) * knh, axis=-1, keepdims=True)
            m_old = m_ref[h][:, :1]
            m_f = jnp.maximum(m_old, s_new)
            alpha = jnp.exp(m_old - m_f)
            pn = jnp.exp(s_new - m_f)
            l_f = l_ref[h][:, :1] * alpha + pn
            o = (acc_ref[h] * alpha + pn * vnh) / l_f
            od = o[0:1, :] - lam * o[1:2, :]
            od = od * lax.rsqrt(jnp.mean(od * od, axis=-1, keepdims=True) + SUBLN_EPS) * sg_ref[...]
            gate = g_ref[:, h * LANE:(h + 1) * LANE]
            o_ref[:, h * LANE:(h + 1) * LANE] = od * (1.0 - lam_init) * _silu(gate)


def _diff_decode(page_flat, q, kn, vn, g, cache_k, cache_v, layer, subln_g, lq1, lk1, lq2, lk2,
                 lam_init):
    n = q.shape[0]
    full = pl.BlockSpec((None, 1, W_B), lambda b, s, pt: (b, 0, 0))
    small = pl.BlockSpec((1, D_B), lambda b, s, pt: (0, 0))
    seq = lambda x: x.reshape(n, 1, W_B)

    def page_spec(j):
        return pl.BlockSpec((None, None, H_B, PAGE, 2 * D_B),
                            lambda b, s, pt: (layer, pt[b * N_PAGES + s * PP + j], 0, 0, 0))

    grid_spec = pltpu.PrefetchScalarGridSpec(
        num_scalar_prefetch=1,
        grid=(n, NSTEP),
        in_specs=[full, full, full, full, pl.BlockSpec((1, LANE), lambda b, s, pt: (0, 0)),
                  small, small, small, small]
                 + [page_spec(j) for j in range(PP)] + [page_spec(j) for j in range(PP)],
        out_specs=full,
        scratch_shapes=[pltpu.VMEM((H_B, 8, LANE), F32)] * 3,
    )
    return pl.pallas_call(
        functools.partial(_diff_dec_kernel, lam_init=lam_init),
        grid_spec=grid_spec,
        out_shape=jax.ShapeDtypeStruct((n, 1, W_B), F32),
        compiler_params=_cparams(("arbitrary", "arbitrary")),
        name="diff_decode",
    )(page_flat, seq(q), seq(kn), seq(vn), seq(g), subln_g.reshape(1, -1), lq1.reshape(1, -1),
      lk1.reshape(1, -1), lq2.reshape(1, -1), lk2.reshape(1, -1),
      *([cache_k] * PP), *([cache_v] * PP)).reshape(n, W_B)


def _fox_dec_kernel(pt_ref, q_ref, kn_ref, vn_ref, g_ref, lfn_ref, tri_ref, *rest):
    kp = rest[:PP]
    vp = rest[PP:2 * PP]
    lfp = rest[2 * PP:3 * PP]
    o_ref, m_ref, l_ref, c_ref, acc_ref = rest[3 * PP:]
    b = pl.program_id(0)
    s = pl.program_id(1)

    @pl.when(s == 0)
    def _():
        m_ref[...] = jnp.full_like(m_ref, NEG)
        l_ref[...] = jnp.zeros_like(l_ref)
        c_ref[...] = jnp.zeros_like(c_ref)
        acc_ref[...] = jnp.zeros_like(acc_ref)

    qrow = q_ref[...] * (D_C ** -0.5)
    row = lax.broadcasted_iota(jnp.int32, (8, D_C), 0)
    qms = [jnp.where(row == (h % 8), qrow[:, h * D_C:(h + 1) * D_C], 0.0).astype(BF16)
           for h in range(H_C)]
    tri = tri_ref[...]
    carry = c_ref[...][:, :1]
    row16 = lax.broadcasted_iota(jnp.int32, (16, PAGE), 0)
    tiles = []
    for j in range(PP):
        t0 = jnp.dot(qms[0], kp[j][0].astype(BF16), preferred_element_type=F32)
        for h in range(1, 8):
            t0 = t0 + jnp.dot(qms[h], kp[j][h].astype(BF16), preferred_element_type=F32)
        t1 = jnp.dot(qms[8], kp[j][8].astype(BF16), preferred_element_type=F32)
        for h in range(9, H_C):
            t1 = t1 + jnp.dot(qms[h], kp[j][h].astype(BF16), preferred_element_type=F32)
        sub = pt_ref[b * N_PAGES + s * PP + j] % 8
        lf = jnp.zeros((16, PAGE), F32)
        for h in range(H_C):
            lf = jnp.where(row16 == h, lfp[j][h, pl.ds(sub, 1), :], lf)
        cum = jnp.dot(lf, tri, precision=HI, preferred_element_type=F32) + carry
        carry = carry + jnp.sum(lf, axis=-1, keepdims=True)
        tiles.append(jnp.concatenate([t0, t1], axis=0) - cum)
    sc = jnp.concatenate(tiles, axis=1)
    p, alpha, m_new, l_new = _online(sc, m_ref[...][:, :1], l_ref[...][:, :1])
    m_ref[...] = jnp.broadcast_to(m_new, (16, LANE))
    l_ref[...] = jnp.broadcast_to(l_new, (16, LANE))
    c_ref[...] = jnp.broadcast_to(carry, (16, LANE))
    for h in range(H_C):
        g8 = 8 * (h // 8)
        pv = _dot_nt(p[g8:g8 + 8, 0:PAGE].astype(BF16), vp[0][h].astype(BF16))
        for j in range(1, PP):
            pv = pv + _dot_nt(p[g8:g8 + 8, j * PAGE:(j + 1) * PAGE].astype(BF16),
                              vp[j][h].astype(BF16))
        acc_ref[h] = acc_ref[h] * alpha[g8:g8 + 8, :] + pv

    @pl.when(s == NSTEP - 1)
    def _():
        sn = []
        for h in range(H_C):
            knh = kn_ref[:, h * D_C:(h + 1) * D_C].astype(BF16).astype(F32)
            sn.append(jnp.sum(qms[h].astype(F32) * knh, axis=-1, keepdims=True))
        t0 = sn[0]
        for h in range(1, 8):
            t0 = t0 + sn[h]
        t1 = sn[8]
        for h in range(9, H_C):
            t1 = t1 + sn[h]
        cum_new = carry + lfn_ref[...][:, :1]
        s_new = jnp.concatenate([t0, t1], axis=0) - cum_new
        m_f = jnp.maximum(m_new, s_new)
        al = jnp.exp(m_new - m_f)
        pn = jnp.exp(s_new - m_f)
        l_f = l_new * al + pn
        outs = []
        for h in range(H_C):
            g8 = 8 * (h // 8)
            vnh = vn_ref[:, h * D_C:(h + 1) * D_C]
            o = (acc_ref[h] * al[g8:g8 + 8, :] + pn[g8:g8 + 8, :] * vnh) / l_f[g8:g8 + 8, :]
            outs.append(o[h % 8:h % 8 + 1, :])
        o_all = jnp.concatenate(outs, axis=1)
        o_ref[...] = o_all * _silu(g_ref[...])


def _fox_decode(page_flat, q, kn, vn, g, lfn_rep, cache_k, cache_v, cache_lft, layer):
    n = q.shape[0]
    full = pl.BlockSpec((None, 1, W_C), lambda b, s, pt: (b, 0, 0))
    seq = lambda x: x.reshape(n, 1, W_C)
    tri = (jnp.arange(PAGE)[:, None] <= jnp.arange(PAGE)[None, :]).astype(F32)

    def page_spec(j):
        return pl.BlockSpec((None, None, H_C, D_C, PAGE),
                            lambda b, s, pt: (layer, pt[b * N_PAGES + s * PP + j], 0, 0, 0))

    def lf_spec(j):
        return pl.BlockSpec((None, H_C, None, 8, PAGE),
                            lambda b, s, pt: (layer, 0, pt[b * N_PAGES + s * PP + j] // 8, 0, 0))

    grid_spec = pltpu.PrefetchScalarGridSpec(
        num_scalar_prefetch=1,
        grid=(n, NSTEP),
        in_specs=[full, full, full, full,
                  pl.BlockSpec((None, 16, LANE), lambda b, s, pt: (b, 0, 0)),
                  pl.BlockSpec((PAGE, PAGE), lambda b, s, pt: (0, 0))]
                 + [page_spec(j) for j in range(PP)] + [page_spec(j) for j in range(PP)]
                 + [lf_spec(j) for j in range(PP)],
        out_specs=full,
        scratch_shapes=[pltpu.VMEM((16, LANE), F32)] * 3 + [pltpu.VMEM((H_C, 8, D_C), F32)],
    )
    return pl.pallas_call(
        _fox_dec_kernel,
        grid_spec=grid_spec,
        out_shape=jax.ShapeDtypeStruct((n, 1, W_C), F32),
        compiler_params=_cparams(("arbitrary", "arbitrary")),
        name="fox_decode",
    )(page_flat, seq(q), seq(kn), seq(vn), seq(g), lfn_rep, tri, *([cache_k] * PP),
      *([cache_v] * PP), *([cache_lft] * PP)).reshape(n, W_C)


def _merge_kernel(ga_ref, gb_ref, gc_ref, ag_ref, y_ref, bonus_ref, gng_ref, gnb_ref, bd_ref,
                  yb_ref, yc_ref, w_ref, o_ref):
    y = y_ref[...]
    bd = bd_ref[...]
    mean = jnp.dot(y, bd, precision=HI, preferred_element_type=F32) * (1.0 / N_A)
    d = y - mean
    var = jnp.dot(d * d, bd, precision=HI, preferred_element_type=F32) * (1.0 / N_A)
    yn = d * lax.rsqrt(var + GN_EPS) * gng_ref[...] + gnb_ref[...] + bonus_ref[...]
    ya = yn * _silu(ag_ref[...])
    pa = jnp.dot(ya.astype(BF16), w_ref[0:W_A, :], preferred_element_type=F32)
    pb = jnp.dot(yb_ref[...].astype(BF16), w_ref[W_A:W_A + W_B, :], preferred_element_type=F32)
    pc = jnp.dot(yc_ref[...].astype(BF16), w_ref[W_A + W_B:, :], preferred_element_type=F32)
    merged = _sigmoid(ga_ref[...]) * pa + _sigmoid(gb_ref[...]) * pb + _sigmoid(gc_ref[...]) * pc
    o_ref[...] = merged.astype(BF16)


def _merge(proj, y_raw, bonus, gn_g, gn_b, bd, yb, yc, w_br, tm):
    m = proj.shape[0]
    gate = lambda c: pl.BlockSpec((tm, D_MODEL), lambda i: (i, c))
    tok = lambda n: pl.BlockSpec((tm, n), lambda i: (i, 0))
    row = pl.BlockSpec((1, W_A), lambda i: (0, 0))
    return pl.pallas_call(
        _merge_kernel,
        grid=(m // tm,),
        in_specs=[gate(0), gate(1), gate(2),
                  pl.BlockSpec((tm, W_A), lambda i: (i, C_AGATE // W_A)),
                  tok(W_A), tok(W_A), row, row, pl.BlockSpec((W_A, W_A), lambda i: (0, 0)),
                  tok(W_B), tok(W_C), pl.BlockSpec((D_MODEL, D_MODEL), lambda i: (0, 0))],
        out_specs=tok(D_MODEL),
        out_shape=jax.ShapeDtypeStruct((m, D_MODEL), BF16),
        compiler_params=_cparams(("parallel",)),
        name="merge",
    )(proj, proj, proj, proj, y_raw, bonus, gn_g.reshape(1, -1), gn_b.reshape(1, -1), bd, yb, yc, w_br)


def _outproj_kernel(x_ref, m_ref, w_ref, o_ref):
    o_ref[...] = x_ref[...] + jnp.dot(m_ref[...], w_ref[...], preferred_element_type=F32)


def _outproj(x2d, merged, w_out, tm):
    m = x2d.shape[0]
    tok = pl.BlockSpec((tm, D_MODEL), lambda i: (i, 0))
    return pl.pallas_call(
        _outproj_kernel,
        grid=(m // tm,),
        in_specs=[tok, tok, pl.BlockSpec((D_MODEL, D_MODEL), lambda i: (0, 0))],
        out_specs=tok,
        out_shape=jax.ShapeDtypeStruct((m, D_MODEL), F32),
        compiler_params=_cparams(("parallel",)),
        name="outproj",
    )(x2d, merged, w_out)


def _rmsnorm_kernel(x_ref, g_ref, o_ref):
    x = x_ref[...]
    o_ref[...] = x * lax.rsqrt(jnp.mean(x * x, axis=-1, keepdims=True) + NORM_EPS) * g_ref[...]


def _rmsnorm(x2d, g, tm):
    m = x2d.shape[0]
    tok = pl.BlockSpec((tm, D_MODEL), lambda i: (i, 0))
    return pl.pallas_call(
        _rmsnorm_kernel,
        grid=(m // tm,),
        in_specs=[tok, pl.BlockSpec((1, D_MODEL), lambda i: (0, 0))],
        out_specs=tok,
        out_shape=jax.ShapeDtypeStruct((m, D_MODEL), F32),
        compiler_params=_cparams(("parallel",)),
        name="final_norm",
    )(x2d, g.reshape(1, -1))


def _permute_w_in(wt):
    pad = jnp.zeros((LANE - H_C, D_MODEL), wt.dtype)
    return jnp.concatenate([wt[8332:14476], wt[1664:2176], wt[0:1664], wt[2176:7552],
                            wt[7564:8332], wt[7552:7564], pad], axis=0).astype(BF16)


def kernel(x_prompt, x_sample, cache_diff_k, cache_diff_v, cache_fox_k, cache_fox_v, cache_fox_logf,
           state_rwkv_wkv, state_rwkv_shift, page_table, norm_g, w_in, rwkv_mu, rwkv_w0, rwkv_w2,
           rwkv_a0, rwkv_a2, rwkv_k_k, rwkv_k_a, rwkv_r_k, rwkv_gn_g, rwkv_gn_b, diff_lq1, diff_lk1,
           diff_lq2, diff_lk2, diff_subln_g, fox_b_f, w_branch, w_out, final_g):
    bsz, t, _ = x_prompt.shape
    n = x_sample.shape[0]
    mp = bsz * t
    hp = x_prompt.reshape(mp, D_MODEL)
    hs = x_sample.reshape(n, D_MODEL)
    page_flat = page_table.reshape(-1).astype(jnp.int32)
    bd = jnp.kron(jnp.eye(H_A, dtype=F32), jnp.ones((N_A, N_A), F32))
    n_pool = cache_fox_logf.shape[1]
    assert n_pool % 8 == 0 and n == LANE
    dk_all = jnp.transpose(cache_diff_k, (0, 1, 3, 2, 4))
    dv_all = jnp.transpose(cache_diff_v, (0, 1, 3, 2, 4))
    fk_all = jnp.transpose(cache_fox_k, (0, 1, 3, 4, 2))
    fv_all = jnp.transpose(cache_fox_v, (0, 1, 3, 4, 2))
    lft_all = jnp.transpose(cache_fox_logf, (0, 3, 1, 2)).reshape(DEPTH, H_C, n_pool // 8, 8, PAGE)
    state_t = jnp.transpose(state_rwkv_wkv, (0, 2, 3, 4, 1))
    zero_lora = jnp.zeros((64, W_A), BF16)
    w_in_t = jnp.transpose(w_in, (2, 0, 1))
    outs_p, outs_s = [], []
    for l in range(DEPTH):
        lam_init = 0.8 - 0.6 * math.exp(-0.3 * l)
        w_p = _permute_w_in(w_in_t[:, l, :])
        w2p = jnp.concatenate([rwkv_w2[l].astype(BF16), zero_lora], axis=0)
        a2p = jnp.concatenate([zero_lora, rwkv_a2[l].astype(BF16)], axis=0)
        bf_pad = jnp.pad(fox_b_f[l], (0, LANE - H_C)).reshape(1, LANE)
        w_br = w_branch[l].astype(BF16)
        w_o = w_out[l].astype(BF16)
        rk = rwkv_r_k[l].reshape(-1)

        proj = _inproj(hp, norm_g[l], w_p, tm=1024)
        u = proj[:, C_U:C_U + A_SHIFT_W].reshape(bsz, t, A_SHIFT_W)
        prev = jnp.concatenate([jnp.zeros((bsz, 1, A_SHIFT_W), F32), u[:, :-1]], axis=1)
        r, dec, k2, v, kk, bb, bonus = _rwkv_prep(
            proj, prev.reshape(mp, A_SHIFT_W), rwkv_mu[l], rwkv_w0[l], rwkv_a0[l], rwkv_k_k[l],
            rwkv_k_a[l], rk, w2p, a2p, bd, tm=512)
        y_t, s_fin = _rwkv_scan(
            _to_scan_tiles(dec, bsz, t), _to_scan_tiles(kk, bsz, t), _to_scan_tiles(bb, bsz, t),
            _to_scan_tiles(k2, bsz, t), _to_scan_tiles(r, bsz, t), _to_scan_v(v, bsz, t),
            jnp.zeros((V_HI, N_A, LANE), F32))
        y_raw = _from_scan_y(y_t, bsz, t)
        yb, dk_new, dv_new = _diff_attn(proj, bsz, t, diff_subln_g[l], diff_lq1[l], diff_lk1[l],
                                        diff_lq2[l], diff_lk2[l], lam_init)
        _, cum, cumt, lft = _foxcum(proj, bsz, t, bf_pad, tm=256)
        yc, fk_new, fv_new = _fox_attn(proj, cum, cumt, bsz, t)
        merged = _merge(proj, y_raw, bonus, rwkv_gn_g[l], rwkv_gn_b[l], bd, yb, yc, w_br, tm=256)
        hp = _outproj(hp, merged, w_o, tm=512)
        outs_p.append((
            jnp.transpose(dk_new, (0, 2, 1, 3)),
            jnp.transpose(dv_new, (0, 2, 1, 3)),
            jnp.transpose(fk_new.reshape(bsz, H_C, D_C, t), (0, 3, 1, 2)),
            jnp.transpose(fv_new.reshape(bsz, H_C, D_C, t), (0, 3, 1, 2)),
            jnp.transpose(lft[:, :H_C, :], (0, 2, 1)),
            _from_scan_state(s_fin, bsz),
            u[:, -1]))

        projs = _inproj(hs, norm_g[l], w_p, tm=n)
        us = projs[:, C_U:C_U + A_SHIFT_W]
        r, dec, k2, v, kk, bb, bonus = _rwkv_prep(
            projs, state_rwkv_shift[l], rwkv_mu[l], rwkv_w0[l], rwkv_a0[l], rwkv_k_k[l],
            rwkv_k_a[l], rk, w2p, a2p, bd, tm=n)
        s_new, y_t = _rwkv_step(state_t, l, dec, kk, bb, k2, r, v)
        y_raw = y_t.T
        lfs, _, _, _ = _foxcum(projs, 1, n, bf_pad, tm=n)
        q_b = projs[:, C_BQ:C_BQ + W_B]
        k_b = projs[:, C_BK:C_BK + W_B]
        v_b = projs[:, C_BV:C_BV + W_B]
        g_b = projs[:, C_BG:C_BG + W_B]
        yb = _diff_decode(page_flat, q_b, k_b, v_b, g_b, dk_all, dv_all, l,
                          diff_subln_g[l], diff_lq1[l], diff_lk1[l], diff_lq2[l], diff_lk2[l], lam_init)
        q_c = projs[:, C_CQ:C_CQ + W_C]
        k_c = projs[:, C_CK:C_CK + W_C]
        v_c = projs[:, C_CV:C_CV + W_C]
        g_c = projs[:, C_CG:C_CG + W_C]
        lfn_rep = jnp.broadcast_to(lfs[:, :16, None], (n, 16, LANE))
        yc = _fox_decode(page_flat, q_c, k_c, v_c, g_c, lfn_rep, fk_all, fv_all, lft_all, l)
        merged = _merge(projs, y_raw, bonus, rwkv_gn_g[l], rwkv_gn_b[l], bd, yb, yc, w_br, tm=n)
        hs = _outproj(hs, merged, w_o, tm=n)
        outs_s.append((
            k_b.reshape(n, 1, H_B, 2 * D_B), v_b.reshape(n, 1, H_B, 2 * D_B),
            k_c.reshape(n, 1, H_C, D_C), v_c.reshape(n, 1, H_C, D_C),
            lfs[:, :H_C].reshape(n, 1, H_C),
            jnp.transpose(s_new, (3, 0, 1, 2)),
            us))

    y_prompt = _rmsnorm(hp, final_g, tm=512).reshape(bsz, t, D_MODEL)
    y_sample = _rmsnorm(hs, final_g, tm=n).reshape(n, 1, D_MODEL)
    st_p = [jnp.stack([o[i] for o in outs_p]) for i in range(7)]
    st_s = [jnp.stack([o[i] for o in outs_s]) for i in range(7)]
    return (y_prompt, y_sample, st_p[0], st_s[0], st_p[1], st_s[1], st_p[2], st_s[2], st_p[3], st_s[3],
            st_p[4], st_s[4], st_p[5], st_s[5], st_p[6], st_s[6])
```

```python
import functools
import math

import jax
import jax.numpy as jnp
from jax import lax
from jax.experimental import pallas as pl
from jax.experimental.pallas import tpu as pltpu

F32 = jnp.float32
BF16 = jnp.bfloat16
HI = lax.Precision.HIGHEST

D_MODEL = 2048
DEPTH = 2
PAGE = 128
H_A, N_A = 8, 64
W_A = H_A * N_A
A_SHIFT_W = 3 * W_A + 128
H_B, D_B = 6, 64
W_B = H_B * 2 * D_B
H_C, D_C = 12, 64
W_C = H_C * D_C
NORM_EPS = 1e-6
SUBLN_EPS = 1e-5
GN_EPS = 64e-5
LANE = 128

C_GATES = 0
C_AGATE = 6144
C_U = 6656
C_BQ, C_BK, C_BV, C_BG = 8320, 9088, 9856, 10624
C_CQ, C_CK, C_CV, C_CG = 11392, 12160, 12928, 13696
C_CF = 14464
NP = 14592
TN_IN = 768

NEG = -1e30


def _cparams(sem, vmem_mb=48):
    return pltpu.CompilerParams(dimension_semantics=sem, vmem_limit_bytes=vmem_mb * 1024 * 1024)


def _softplus(z):
    return jnp.maximum(z, 0.0) + jnp.log1p(jnp.exp(-jnp.abs(z)))


def _sigmoid(z):
    return jax.nn.sigmoid(z)


def _silu(z):
    return z * jax.nn.sigmoid(z)


def _dot_nt(a, b):
    return lax.dot_general(a, b, (((1,), (1,)), ((), ())), preferred_element_type=F32)


def _inproj_kernel(x_ref, g_ref, w_ref, o_ref, h_ref):
    @pl.when(pl.program_id(1) == 0)
    def _():
        x = x_ref[...]
        ms = jnp.mean(x * x, axis=-1, keepdims=True)
        h_ref[...] = (x * lax.rsqrt(ms + NORM_EPS) * g_ref[...]).astype(BF16)

    o_ref[...] = _dot_nt(h_ref[...], w_ref[...])


def _inproj(x2d, g, w_p, tm):
    m = x2d.shape[0]
    return pl.pallas_call(
        _inproj_kernel,
        grid=(m // tm, NP // TN_IN),
        in_specs=[pl.BlockSpec((tm, D_MODEL), lambda i, j: (i, 0)),
                  pl.BlockSpec((1, D_MODEL), lambda i, j: (0, 0)),
                  pl.BlockSpec((TN_IN, D_MODEL), lambda i, j: (j, 0))],
        out_specs=pl.BlockSpec((tm, TN_IN), lambda i, j: (i, j)),
        out_shape=jax.ShapeDtypeStruct((m, NP), F32),
        scratch_shapes=[pltpu.VMEM((tm, D_MODEL), BF16)],
        compiler_params=_cparams(("parallel", "arbitrary")),
        name="inproj",
    )(x2d, g.reshape(1, D_MODEL), w_p)


def _rwkv_prep_kernel(u_ref, prev_ref, mu_ref, w0_ref, a0_ref, kk_ref, ka_ref, rk_ref,
                      w2_ref, a2_ref, bd_ref,
                      r_o, w_o, k_o, v_o, kk_o, b_o, bonus_o, *, blocks_per_seq):
    u = u_ref[...]
    if blocks_per_seq is None:
        prev = prev_ref[...]
    else:
        tm = u.shape[0]
        carry_in = jnp.where(pl.program_id(0) % blocks_per_seq == 0, 0.0, prev_ref[7:8, :])
        row = lax.broadcasted_iota(jnp.int32, (tm, 1), 0)
        prev = jnp.where(row == 0, carry_in, pltpu.roll(u, 1, axis=0))
    m = u + mu_ref[...] * (prev - u)
    r = m[:, 0:W_A]
    k = m[:, W_A:2 * W_A]
    v = m[:, 2 * W_A:3 * W_A]
    lo = m[:, 3 * W_A:3 * W_A + 128]
    zw = w0_ref[...] + jnp.dot(jnp.tanh(lo).astype(BF16), w2_ref[...], preferred_element_type=F32)
    w_log = -_softplus(-zw) - 0.5
    a = _sigmoid(a0_ref[...] + jnp.dot(lo.astype(BF16), a2_ref[...], preferred_element_type=F32))
    decay = jnp.exp(-jnp.exp(w_log))
    bd = bd_ref[...]
    kk = k * kk_ref[...]
    ss = jnp.dot(kk * kk, bd, precision=HI, preferred_element_type=F32)
    kk = kk * lax.rsqrt(jnp.maximum(ss, 1e-24))
    k2 = k * (1.0 + (a - 1.0) * ka_ref[...])
    bonus = jnp.dot(r * k2 * rk_ref[...], bd, precision=HI, preferred_element_type=F32) * v
    r_o[...] = r
    w_o[...] = decay
    k_o[...] = k2
    v_o[...] = v
    kk_o[...] = kk
    b_o[...] = kk * a
    bonus_o[...] = bonus


def _rwkv_prep(proj, prev, mu, w0, a0, k_k, k_a, r_k, w2p, a2p, bd, tm, seq_len=None):
    m = proj.shape[0]
    row = lambda n: pl.BlockSpec((1, n), lambda i: (0, 0))
    full = lambda a: pl.BlockSpec(a.shape, lambda i: (0, 0))
    tok = pl.BlockSpec((tm, W_A), lambda i: (i, 0))
    if prev is None:
        blocks_per_seq = seq_len // tm
        prev = proj
        prev_spec = pl.BlockSpec((8, A_SHIFT_W),
                                 lambda i: (jnp.maximum(i * (tm // 8) - 1, 0), C_U // A_SHIFT_W))
    else:
        blocks_per_seq = None
        prev_spec = pl.BlockSpec((tm, A_SHIFT_W), lambda i: (i, 0))
    return pl.pallas_call(
        functools.partial(_rwkv_prep_kernel, blocks_per_seq=blocks_per_seq),
        grid=(m // tm,),
        in_specs=[pl.BlockSpec((tm, A_SHIFT_W), lambda i: (i, C_U // A_SHIFT_W)),
                  prev_spec,
                  row(A_SHIFT_W), row(W_A), row(W_A), row(W_A), row(W_A), row(W_A),
                  full(w2p), full(a2p), full(bd)],
        out_specs=[tok] * 7,
        out_shape=[jax.ShapeDtypeStruct((m, W_A), F32)] * 7,
        compiler_params=_cparams(("parallel",)),
        name="rwkv_prep",
    )(proj, prev, mu.reshape(1, -1), w0.reshape(1, -1), a0.reshape(1, -1), k_k.reshape(1, -1),
      k_a.reshape(1, -1), r_k.reshape(1, -1), w2p, a2p, bd)


V_HI = 16
SCAN_TC = 16


def _rwkv_scan_kernel(w_ref, kk_ref, b_ref, k_ref, r_ref, v_ref, s0_ref, y_ref, s_ref):
    @pl.when(pl.program_id(0) == 0)
    def _():
        s_ref[...] = s0_ref[...]

    def step(t, carry):
        w = w_ref[t]
        kk = kk_ref[t]
        b = b_ref[t]
        kv = k_ref[t]
        r = r_ref[t]
        vt = v_ref[t]
        rows = []
        for vh in range(V_HI):
            s = s_ref[vh]
            sk = jnp.sum(s * kk, axis=0, keepdims=True)
            s2 = s * w - sk * b + vt[vh:vh + 1, :] * kv
            s_ref[vh] = s2
            rows.append(jnp.sum(s2 * r, axis=0, keepdims=True))
        y_ref[t] = jnp.concatenate(rows, axis=0)
        return carry

    lax.fori_loop(0, SCAN_TC, step, 0)


def _rwkv_scan(w_t, kk_t, b_t, k_t, r_t, v_t, s0):
    t = w_t.shape[0]
    tile = pl.BlockSpec((SCAN_TC, N_A, LANE), lambda i: (i, 0, 0))
    vspec = pl.BlockSpec((SCAN_TC, V_HI, LANE), lambda i: (i, 0, 0))
    sspec = pl.BlockSpec((V_HI, N_A, LANE), lambda i: (0, 0, 0))
    return pl.pallas_call(
        _rwkv_scan_kernel,
        grid=(t // SCAN_TC,),
        in_specs=[tile] * 5 + [vspec, sspec],
        out_specs=[vspec, sspec],
        out_shape=[jax.ShapeDtypeStruct((t, V_HI, LANE), F32),
                   jax.ShapeDtypeStruct((V_HI, N_A, LANE), F32)],
        compiler_params=_cparams(("arbitrary",)),
        name="rwkv_scan",
    )(w_t, kk_t, b_t, k_t, r_t, v_t, s0)


def _to_scan_tiles(x, bsz, t):
    x = x.reshape(bsz, t, H_A, N_A).transpose(1, 3, 0, 2).reshape(t, N_A, 1, bsz * H_A)
    return jnp.broadcast_to(x, (t, N_A, LANE // (bsz * H_A), bsz * H_A)).reshape(t, N_A, LANE)


def _to_scan_v(v, bsz, t):
    v = v.reshape(bsz, t, H_A, V_HI, N_A // V_HI).transpose(1, 3, 4, 0, 2)
    return v.reshape(t, V_HI, LANE)


def _from_scan_y(y, bsz, t):
    y = y.reshape(t, V_HI, N_A // V_HI, bsz, H_A).transpose(3, 0, 4, 1, 2)
    return y.reshape(bsz * t, W_A)


def _from_scan_state(s, bsz):
    s = s.reshape(V_HI, N_A, N_A // V_HI, bsz, H_A).transpose(3, 4, 0, 2, 1)
    return s.reshape(bsz, H_A, N_A, N_A)


def _rwkv_step_kernel(s_ref, w_ref, kk_ref, b_ref, k_ref, r_ref, v_ref, so_ref, y_ref):
    w = w_ref[...]
    kk = kk_ref[...]
    b = b_ref[...]
    kv = k_ref[...]
    r = r_ref[...]

    def body(v, carry):
        s = s_ref[v]
        sk = jnp.sum(s * kk, axis=0, keepdims=True)
        s2 = s * w - sk * b + v_ref[pl.ds(v, 1), :] * kv
        so_ref[v] = s2
        y_ref[pl.ds(v, 1), :] = jnp.sum(s2 * r, axis=0, keepdims=True)
        return carry

    lax.fori_loop(0, N_A, body, 0, unroll=4)


def _rwkv_step(state_t, layer, w, kk, b, k2, r, v):
    n = w.shape[0]
    tile = pl.BlockSpec((N_A, n), lambda h: (h, 0))
    return pl.pallas_call(
        _rwkv_step_kernel,
        grid=(H_A,),
        in_specs=[pl.BlockSpec((None, None, N_A, N_A, n), lambda h: (layer, h, 0, 0, 0)),
                  tile, tile, tile, tile, tile, tile],
        out_specs=[pl.BlockSpec((None, N_A, N_A, n), lambda h: (h, 0, 0, 0)), tile],
        out_shape=[jax.ShapeDtypeStruct((H_A, N_A, N_A, n), F32),
                   jax.ShapeDtypeStruct((W_A, n), F32)],
        compiler_params=_cparams(("parallel",)),
        name="rwkv_step",
    )(state_t, w.T, kk.T, b.T, k2.T, r.T, v.T)


def _lam(lq1, lk1, lq2, lk2, lam_init):
    return (jnp.exp(jnp.sum(lq1[...] * lk1[...], axis=-1, keepdims=True))
            - jnp.exp(jnp.sum(lq2[...] * lk2[...], axis=-1, keepdims=True)) + lam_init)


def _online(s, m, l):
    m_new = jnp.maximum(m, jnp.max(s, axis=-1, keepdims=True))
    alpha = jnp.exp(m - m_new)
    p = jnp.exp(s - m_new)
    return p, alpha, m_new, l * alpha + jnp.sum(p, axis=-1, keepdims=True)


def _diff_attn_kernel(q_ref, k_ref, v_ref, g_ref, sg_ref, lq1, lk1, lq2, lk2, o_ref, ko_ref, vo_ref,
                      *, lam_init, tq, tk):
    qi = pl.program_id(2)

    @pl.when(qi == 0)
    def _():
        ko_ref[...] = k_ref[...]
        vo_ref[...] = v_ref[...]

    lane = lax.broadcasted_iota(jnp.int32, (1, LANE), 1)
    q = q_ref[...] * (D_B ** -0.5)
    q1 = jnp.where(lane < D_B, q, 0.0).astype(BF16)
    q2 = jnp.where(lane >= D_B, q, 0.0).astype(BF16)
    rows = qi * tq + lax.broadcasted_iota(jnp.int32, (tq, tk), 0)
    cols0 = lax.broadcasted_iota(jnp.int32, (tq, tk), 1)

    def body(j, carry):
        m1, l1, a1, m2, l2, a2 = carry
        off = pl.multiple_of(j * tk, tk)
        kb = k_ref[pl.ds(off, tk), :].astype(BF16)
        vb = v_ref[pl.ds(off, tk), :].astype(BF16)
        mask = (cols0 + off) <= rows
        s1 = jnp.where(mask, _dot_nt(q1, kb), NEG)
        s2 = jnp.where(mask, _dot_nt(q2, kb), NEG)
        p1, al1, m1, l1 = _online(s1, m1, l1)
        p2, al2, m2, l2 = _online(s2, m2, l2)
        a1 = a1 * al1 + jnp.dot(p1.astype(BF16), vb, preferred_element_type=F32)
        a2 = a2 * al2 + jnp.dot(p2.astype(BF16), vb, preferred_element_type=F32)
        return m1, l1, a1, m2, l2, a2

    mi = jnp.full((tq, 1), NEG, F32)
    zi = jnp.zeros((tq, 1), F32)
    ai = jnp.zeros((tq, LANE), F32)
    nkv = (qi * tq + tq + tk - 1) // tk
    m1, l1, a1, m2, l2, a2 = lax.fori_loop(0, nkv, body, (mi, zi, ai, mi, zi, ai))
    lam = _lam(lq1, lk1, lq2, lk2, lam_init)
    o = a1 / l1 - lam * (a2 / l2)
    o = o * lax.rsqrt(jnp.mean(o * o, axis=-1, keepdims=True) + SUBLN_EPS) * sg_ref[...]
    o_ref[...] = o * (1.0 - lam_init) * _silu(g_ref[...])


def _diff_attn(proj, bsz, t, subln_g, lq1, lk1, lq2, lk2, lam_init, tq=512, tk=512):
    nq = t // tq
    small = pl.BlockSpec((1, D_B), lambda b, h, i: (0, 0))
    return pl.pallas_call(
        functools.partial(_diff_attn_kernel, lam_init=lam_init, tq=tq, tk=tk),
        grid=(bsz, H_B, nq),
        in_specs=[pl.BlockSpec((tq, LANE), lambda b, h, i: (b * nq + i, C_BQ // LANE + h)),
                  pl.BlockSpec((t, LANE), lambda b, h, i: (b, C_BK // LANE + h)),
                  pl.BlockSpec((t, LANE), lambda b, h, i: (b, C_BV // LANE + h)),
                  pl.BlockSpec((tq, LANE), lambda b, h, i: (b * nq + i, C_BG // LANE + h)),
                  pl.BlockSpec((1, LANE), lambda b, h, i: (0, 0)),
                  small, small, small, small],
        out_specs=[pl.BlockSpec((tq, LANE), lambda b, h, i: (b * nq + i, h)),
                   pl.BlockSpec((None, None, t, LANE), lambda b, h, i: (b, h, 0, 0)),
                   pl.BlockSpec((None, None, t, LANE), lambda b, h, i: (b, h, 0, 0))],
        out_shape=[jax.ShapeDtypeStruct((bsz * t, W_B), F32),
                   jax.ShapeDtypeStruct((bsz, H_B, t, LANE), F32),
                   jax.ShapeDtypeStruct((bsz, H_B, t, LANE), F32)],
        compiler_params=_cparams(("parallel", "parallel", "arbitrary")),
        name="diff_attn",
    )(proj, proj, proj, proj, subln_g.reshape(1, -1), lq1.reshape(1, -1), lk1.reshape(1, -1),
      lq2.reshape(1, -1), lk2.reshape(1, -1))


def _foxcum_kernel(cf_ref, bf_ref, tri_ref, lf_ref, cum_ref, cumt_ref, lft_ref, carry_ref, *, tm):
    @pl.when(pl.program_id(1) == 0)
    def _():
        carry_ref[...] = jnp.zeros_like(carry_ref)

    lf = -_softplus(-(cf_ref[...] + bf_ref[...]))
    lf_ref[...] = lf
    lft_ref[...] = lf.T[:16, :]
    c = jnp.dot(tri_ref[...], lf, precision=HI, preferred_element_type=F32) + carry_ref[...]
    cum_ref[...] = c
    carry_ref[...] = c[tm - 1:tm, :]
    cumt_ref[...] = c.T[:16, :]


def _foxcum(proj, bsz, t, bf_pad, tm):
    nt = t // tm
    tri = (jnp.arange(tm)[:, None] >= jnp.arange(tm)[None, :]).astype(F32)
    tok = pl.BlockSpec((tm, LANE), lambda b, i: (b * nt + i, 0))
    return pl.pallas_call(
        functools.partial(_foxcum_kernel, tm=tm),
        grid=(bsz, nt),
        in_specs=[pl.BlockSpec((tm, LANE), lambda b, i: (b * nt + i, C_CF // LANE)),
                  pl.BlockSpec((1, LANE), lambda b, i: (0, 0)),
                  pl.BlockSpec((tm, tm), lambda b, i: (0, 0))],
        out_specs=[tok, tok, pl.BlockSpec((None, 16, tm), lambda b, i: (b, 0, i)),
                   pl.BlockSpec((None, 16, tm), lambda b, i: (b, 0, i))],
        out_shape=[jax.ShapeDtypeStruct((bsz * t, LANE), F32),
                   jax.ShapeDtypeStruct((bsz * t, LANE), F32),
                   jax.ShapeDtypeStruct((bsz, 16, t), F32),
                   jax.ShapeDtypeStruct((bsz, 16, t), F32)],
        scratch_shapes=[pltpu.VMEM((1, LANE), F32)],
        compiler_params=_cparams(("parallel", "arbitrary")),
        name="fox_cum",
    )(proj, bf_pad, tri)


def _fox_attn_kernel(q_ref, k_ref, v_ref, g_ref, cum_ref, cumt_ref, o_ref, kto_ref, vto_ref, *, tq, tk):
    hp = pl.program_id(1)
    qi = pl.program_id(2)

    @pl.when(qi == 0)
    def _():
        kto_ref[...] = k_ref[...].T
        vto_ref[...] = v_ref[...].T

    lane = lax.broadcasted_iota(jnp.int32, (1, LANE), 1)
    q = q_ref[...] * (D_C ** -0.5)
    qa = jnp.where(lane < D_C, q, 0.0).astype(BF16)
    qb = jnp.where(lane >= D_C, q, 0.0).astype(BF16)
    cq = cum_ref[...]
    cqa = jnp.sum(jnp.where(lane == 2 * hp, cq, 0.0), axis=-1, keepdims=True)
    cqb = jnp.sum(jnp.where(lane == 2 * hp + 1, cq, 0.0), axis=-1, keepdims=True)
    rows = qi * tq + lax.broadcasted_iota(jnp.int32, (tq, tk), 0)
    cols0 = lax.broadcasted_iota(jnp.int32, (tq, tk), 1)

    def body(j, carry):
        ma, la, aa, mb, lb, ab = carry
        off = pl.multiple_of(j * tk, tk)
        kb = k_ref[pl.ds(off, tk), :].astype(BF16)
        vb = v_ref[pl.ds(off, tk), :].astype(BF16)
        cka = cumt_ref[pl.ds(2 * hp, 1), pl.ds(off, tk)]
        ckb = cumt_ref[pl.ds(2 * hp + 1, 1), pl.ds(off, tk)]
        mask = (cols0 + off) <= rows
        sa = jnp.where(mask, _dot_nt(qa, kb) + (cqa - cka), NEG)
        sb = jnp.where(mask, _dot_nt(qb, kb) + (cqb - ckb), NEG)
        pa, ala, ma, la = _online(sa, ma, la)
        pb, alb, mb, lb = _online(sb, mb, lb)
        aa = aa * ala + jnp.dot(pa.astype(BF16), vb, preferred_element_type=F32)
        ab = ab * alb + jnp.dot(pb.astype(BF16), vb, preferred_element_type=F32)
        return ma, la, aa, mb, lb, ab

    mi = jnp.full((tq, 1), NEG, F32)
    zi = jnp.zeros((tq, 1), F32)
    ai = jnp.zeros((tq, LANE), F32)
    nkv = (qi * tq + tq + tk - 1) // tk
    ma, la, aa, mb, lb, ab = lax.fori_loop(0, nkv, body, (mi, zi, ai, mi, zi, ai))
    o = jnp.where(lane < D_C, aa / la, ab / lb)
    o_ref[...] = o * _silu(g_ref[...])


def _fox_attn(proj, cum, cumt, bsz, t, tq=512, tk=512):
    nq = t // tq
    return pl.pallas_call(
        functools.partial(_fox_attn_kernel, tq=tq, tk=tk),
        grid=(bsz, H_C // 2, nq),
        in_specs=[pl.BlockSpec((tq, LANE), lambda b, h, i: (b * nq + i, C_CQ // LANE + h)),
                  pl.BlockSpec((t, LANE), lambda b, h, i: (b, C_CK // LANE + h)),
                  pl.BlockSpec((t, LANE), lambda b, h, i: (b, C_CV // LANE + h)),
                  pl.BlockSpec((tq, LANE), lambda b, h, i: (b * nq + i, C_CG // LANE + h)),
                  pl.BlockSpec((tq, LANE), lambda b, h, i: (b * nq + i, 0)),
                  pl.BlockSpec((None, 16, t), lambda b, h, i: (b, 0, 0))],
        out_specs=[pl.BlockSpec((tq, LANE), lambda b, h, i: (b * nq + i, h)),
                   pl.BlockSpec((None, None, LANE, t), lambda b, h, i: (b, h, 0, 0)),
                   pl.BlockSpec((None, None, LANE, t), lambda b, h, i: (b, h, 0, 0))],
        out_shape=[jax.ShapeDtypeStruct((bsz * t, W_C), F32),
                   jax.ShapeDtypeStruct((bsz, H_C // 2, LANE, t), F32),
                   jax.ShapeDtypeStruct((bsz, H_C // 2, LANE, t), F32)],
        compiler_params=_cparams(("parallel", "parallel", "arbitrary")),
        name="fox_attn",
    )(proj, proj, proj, proj, cum, cumt)


PP = 8
N_PAGES = 16
NSTEP = N_PAGES // PP


def _diff_dec_kernel(pt_ref, q_ref, kn_ref, vn_ref, g_ref, sg_ref, lq1, lk1, lq2, lk2, *rest,
                     lam_init):
    kp = rest[:PP]
    vp = rest[PP:2 * PP]
    o_ref, qrep_ref, m_ref, l_ref, acc_ref = rest[2 * PP:]
    s = pl.program_id(1)

    @pl.when(s == 0)
    def _():
        qrow = q_ref[...] * (D_B ** -0.5)
        r = lax.broadcasted_iota(jnp.int32, (LANE, LANE), 0)
        c = lax.broadcasted_iota(jnp.int32, (LANE, LANE), 1)
        for h in range(H_B):
            qh = qrow[:, h * LANE:(h + 1) * LANE]
            col = jnp.sum(jnp.where(r == c, qh, 0.0), axis=1, keepdims=True)
            colb = jnp.broadcast_to(col, (LANE, LANE))
            qrep_ref[h] = jnp.concatenate([jnp.where(r < D_B, colb, 0.0),
                                           jnp.where(r >= D_B, colb, 0.0)], axis=1).astype(BF16)
        m_ref[...] = jnp.full_like(m_ref, NEG)
        l_ref[...] = jnp.zeros_like(l_ref)
        acc_ref[...] = jnp.zeros_like(acc_ref)

    for h in range(H_B):
        w = qrep_ref[h]
        ss = [jnp.dot(kp[j][h].astype(BF16), w, preferred_element_type=F32) for j in range(PP)]
        for mp in range(2):
            idx = 2 * h + mp
            sm = [x[:, mp * LANE:(mp + 1) * LANE] for x in ss]
            mx = sm[0]
            for j in range(1, PP):
                mx = jnp.maximum(mx, sm[j])
            m_old = m_ref[idx][0:1, :]
            m_new = jnp.maximum(m_old, jnp.max(mx, axis=0, keepdims=True))
            alpha = jnp.exp(m_old - m_new)
            acc = acc_ref[idx] * alpha
            l = l_ref[idx] * alpha
            for j in range(PP):
                p = jnp.exp(sm[j] - m_new)
                acc = acc + p * vp[j][h]
                l = l + p
            acc_ref[idx] = acc
            l_ref[idx] = l
            m_ref[idx] = jnp.broadcast_to(m_new, (8, LANE))

    @pl.when(s == NSTEP - 1)
    def _():
        lam = _lam(lq1, lk1, lq2, lk2, lam_init)
        for h in range(H_B):
            knh = jnp.broadcast_to(kn_ref[:, h * LANE:(h + 1) * LANE], (8, LANE)).astype(BF16)
            sn = jnp.dot(knh, qrep_ref[h], preferred_element_type=F32)[0:1, :]
            vnh = vn_ref[:, h * LANE:(h + 1) * LANE]
            outs = []
            for mp in range(2):
                idx = 2 * h + mp
                s_new = sn[:, mp * LANE:(mp + 1) * LANE]
                m_old = m_ref[idx][0:1, :]
                m_f = jnp.maximum(m_old, s_new)
                alpha = jnp.exp(m_old - m_f)
                pn = jnp.exp(s_new - m_f)
                num = jnp.sum(acc_ref[idx], axis=0, keepdims=True) * alpha + pn * vnh
                den = jnp.sum(l_ref[idx], axis=0, keepdims=True) * alpha + pn
                outs.append(num / den)
            od = outs[0] - lam * outs[1]
            od = od * lax.rsqrt(jnp.mean(od * od, axis=-1, keepdims=True) + SUBLN_EPS) * sg_ref[...]
            gate = g_ref[:, h * LANE:(h + 1) * LANE]
            o_ref[:, h * LANE:(h + 1) * LANE] = od * (1.0 - lam_init) * _silu(gate)


def _diff_decode(page_flat, q, kn, vn, g, cache_k, cache_v, layer, subln_g, lq1, lk1, lq2, lk2,
                 lam_init):
    n = q.shape[0]
    full = pl.BlockSpec((None, 1, W_B), lambda b, s, pt: (b, 0, 0))
    small = pl.BlockSpec((1, D_B), lambda b, s, pt: (0, 0))
    seq = lambda x: x.reshape(n, 1, W_B)

    def page_spec(j):
        return pl.BlockSpec((None, None, H_B, PAGE, 2 * D_B),
                            lambda b, s, pt: (layer, pt[b * N_PAGES + s * PP + j], 0, 0, 0))

    grid_spec = pltpu.PrefetchScalarGridSpec(
        num_scalar_prefetch=1,
        grid=(n, NSTEP),
        in_specs=[full, full, full, full, pl.BlockSpec((1, LANE), lambda b, s, pt: (0, 0)),
                  small, small, small, small]
                 + [page_spec(j) for j in range(PP)] + [page_spec(j) for j in range(PP)],
        out_specs=full,
        scratch_shapes=[pltpu.VMEM((H_B, LANE, 2 * LANE), BF16),
                        pltpu.VMEM((2 * H_B, 8, LANE), F32),
                        pltpu.VMEM((2 * H_B, PAGE, LANE), F32),
                        pltpu.VMEM((2 * H_B, PAGE, LANE), F32)],
    )
    return pl.pallas_call(
        functools.partial(_diff_dec_kernel, lam_init=lam_init),
        grid_spec=grid_spec,
        out_shape=jax.ShapeDtypeStruct((n, 1, W_B), F32),
        compiler_params=_cparams(("arbitrary", "arbitrary")),
        name="diff_decode",
    )(page_flat, seq(q), seq(kn), seq(vn), seq(g), subln_g.reshape(1, -1), lq1.reshape(1, -1),
      lk1.reshape(1, -1), lq2.reshape(1, -1), lk2.reshape(1, -1),
      *([cache_k] * PP), *([cache_v] * PP)).reshape(n, W_B)


def _fox_dec_kernel(pt_ref, q_ref, kn_ref, vn_ref, g_ref, lfn_ref, tri_ref, *rest):
    kp = rest[:PP]
    vp = rest[PP:2 * PP]
    lfp = rest[2 * PP:3 * PP]
    o_ref, qcol_ref, m_ref, l_ref, c_ref, acc_ref = rest[3 * PP:]
    b = pl.program_id(0)
    s = pl.program_id(1)

    @pl.when(s == 0)
    def _():
        qrow = q_ref[...] * (D_C ** -0.5)
        r = lax.broadcasted_iota(jnp.int32, (D_C, D_C), 0)
        c = lax.broadcasted_iota(jnp.int32, (D_C, D_C), 1)
        for h in range(H_C):
            qh = qrow[:, h * D_C:(h + 1) * D_C]
            col = jnp.sum(jnp.where(r == c, qh, 0.0), axis=1, keepdims=True)
            qcol_ref[h] = jnp.broadcast_to(col, (D_C, PAGE))
        m_ref[...] = jnp.full_like(m_ref, NEG)
        l_ref[...] = jnp.zeros_like(l_ref)
        c_ref[...] = jnp.zeros_like(c_ref)
        acc_ref[...] = jnp.zeros_like(acc_ref)

    tri = tri_ref[...]
    carry = c_ref[...][:, :1]
    row16 = lax.broadcasted_iota(jnp.int32, (16, PAGE), 0)
    tiles = []
    for j in range(PP):
        sub = pt_ref[b * N_PAGES + s * PP + j] % 8
        sc_t = jnp.zeros((16, PAGE), F32)
        lf = jnp.zeros((16, PAGE), F32)
        for h in range(H_C):
            sh = jnp.sum(kp[j][h] * qcol_ref[h], axis=0, keepdims=True)
            sc_t = jnp.where(row16 == h, sh, sc_t)
            lf = jnp.where(row16 == h, lfp[j][h, pl.ds(sub, 1), :], lf)
        cum = jnp.dot(lf, tri, precision=HI, preferred_element_type=F32) + carry
        carry = carry + jnp.sum(lf, axis=-1, keepdims=True)
        tiles.append(sc_t - cum)
    sc = jnp.concatenate(tiles, axis=1)
    p, alpha, m_new, l_new = _online(sc, m_ref[...][:, :1], l_ref[...][:, :1])
    m_ref[...] = jnp.broadcast_to(m_new, (16, LANE))
    l_ref[...] = jnp.broadcast_to(l_new, (16, LANE))
    c_ref[...] = jnp.broadcast_to(carry, (16, LANE))
    for h in range(H_C):
        acc = acc_ref[h] * alpha[h:h + 1, :]
        for j in range(PP):
            acc = acc + vp[j][h] * p[h:h + 1, j * PAGE:(j + 1) * PAGE]
        acc_ref[h] = acc

    @pl.when(s == NSTEP - 1)
    def _():
        qrow = q_ref[...] * (D_C ** -0.5)
        rcol = lax.broadcasted_iota(jnp.int32, (16, 1), 0)
        s_new = jnp.zeros((16, 1), F32)
        for h in range(H_C):
            dot = jnp.sum(qrow[:, h * D_C:(h + 1) * D_C] * kn_ref[:, h * D_C:(h + 1) * D_C],
                          axis=-1, keepdims=True)
            s_new = jnp.where(rcol == h, dot, s_new)
        s_new = s_new - (carry + lfn_ref[...][:, :1])
        m_f = jnp.maximum(m_new, s_new)
        al = jnp.exp(m_new - m_f)
        pn = jnp.exp(s_new - m_f)
        l_f = l_new * al + pn
        ones = jnp.ones((8, PAGE), F32)
        outs = []
        for h in range(H_C):
            past = lax.dot_general(ones, acc_ref[h], (((1,), (1,)), ((), ())), precision=HI,
                                   preferred_element_type=F32)[0:1, :]
            vnh = vn_ref[:, h * D_C:(h + 1) * D_C]
            outs.append((past * al[h:h + 1, :] + pn[h:h + 1, :] * vnh) / l_f[h:h + 1, :])
        o_all = jnp.concatenate(outs, axis=1)
        o_ref[...] = o_all * _silu(g_ref[...])


def _fox_decode(page_flat, q, kn, vn, g, lfn_rep, cache_k, cache_v, cache_lft, layer):
    n = q.shape[0]
    full = pl.BlockSpec((None, 1, W_C), lambda b, s, pt: (b, 0, 0))
    seq = lambda x: x.reshape(n, 1, W_C)
    tri = (jnp.arange(PAGE)[:, None] <= jnp.arange(PAGE)[None, :]).astype(F32)

    def page_spec(j):
        return pl.BlockSpec((None, None, H_C, D_C, PAGE),
                            lambda b, s, pt: (layer, pt[b * N_PAGES + s * PP + j], 0, 0, 0))

    def lf_spec(j):
        return pl.BlockSpec((None, H_C, None, 8, PAGE),
                            lambda b, s, pt: (layer, 0, pt[b * N_PAGES + s * PP + j] // 8, 0, 0))

    grid_spec = pltpu.PrefetchScalarGridSpec(
        num_scalar_prefetch=1,
        grid=(n, NSTEP),
        in_specs=[full, full, full, full,
                  pl.BlockSpec((None, 16, LANE), lambda b, s, pt: (b, 0, 0)),
                  pl.BlockSpec((PAGE, PAGE), lambda b, s, pt: (0, 0))]
                 + [page_spec(j) for j in range(PP)] + [page_spec(j) for j in range(PP)]
                 + [lf_spec(j) for j in range(PP)],
        out_specs=full,
        scratch_shapes=[pltpu.VMEM((H_C, D_C, PAGE), F32)] + [pltpu.VMEM((16, LANE), F32)] * 3
                       + [pltpu.VMEM((H_C, D_C, PAGE), F32)],
    )
    return pl.pallas_call(
        _fox_dec_kernel,
        grid_spec=grid_spec,
        out_shape=jax.ShapeDtypeStruct((n, 1, W_C), F32),
        compiler_params=_cparams(("arbitrary", "arbitrary")),
        name="fox_decode",
    )(page_flat, seq(q), seq(kn), seq(vn), seq(g), lfn_rep, tri, *([cache_k] * PP),
      *([cache_v] * PP), *([cache_lft] * PP)).reshape(n, W_C)


def _merge_kernel(ga_ref, gb_ref, gc_ref, ag_ref, y_ref, bonus_ref, gng_ref, gnb_ref, bd_ref,
                  yb_ref, yc_ref, w_ref, o_ref):
    y = y_ref[...]
    bd = bd_ref[...]
    mean = jnp.dot(y, bd, precision=HI, preferred_element_type=F32) * (1.0 / N_A)
    d = y - mean
    var = jnp.dot(d * d, bd, precision=HI, preferred_element_type=F32) * (1.0 / N_A)
    yn = d * lax.rsqrt(var + GN_EPS) * gng_ref[...] + gnb_ref[...] + bonus_ref[...]
    ya = yn * _silu(ag_ref[...])
    pa = jnp.dot(ya.astype(BF16), w_ref[0:W_A, :], preferred_element_type=F32)
    pb = jnp.dot(yb_ref[...].astype(BF16), w_ref[W_A:W_A + W_B, :], preferred_element_type=F32)
    pc = jnp.dot(yc_ref[...].astype(BF16), w_ref[W_A + W_B:, :], preferred_element_type=F32)
    merged = _sigmoid(ga_ref[...]) * pa + _sigmoid(gb_ref[...]) * pb + _sigmoid(gc_ref[...]) * pc
    o_ref[...] = merged.astype(BF16)


def _merge(proj, y_raw, bonus, gn_g, gn_b, bd, yb, yc, w_br, tm):
    m = proj.shape[0]
    gate = lambda c: pl.BlockSpec((tm, D_MODEL), lambda i: (i, c))
    tok = lambda n: pl.BlockSpec((tm, n), lambda i: (i, 0))
    row = pl.BlockSpec((1, W_A), lambda i: (0, 0))
    return pl.pallas_call(
        _merge_kernel,
        grid=(m // tm,),
        in_specs=[gate(0), gate(1), gate(2),
                  pl.BlockSpec((tm, W_A), lambda i: (i, C_AGATE // W_A)),
                  tok(W_A), tok(W_A), row, row, pl.BlockSpec((W_A, W_A), lambda i: (0, 0)),
                  tok(W_B), tok(W_C), pl.BlockSpec((D_MODEL, D_MODEL), lambda i: (0, 0))],
        out_specs=tok(D_MODEL),
        out_shape=jax.ShapeDtypeStruct((m, D_MODEL), BF16),
        compiler_params=_cparams(("parallel",)),
        name="merge",
    )(proj, proj, proj, proj, y_raw, bonus, gn_g.reshape(1, -1), gn_b.reshape(1, -1), bd, yb, yc, w_br)


def _outproj_kernel(x_ref, m_ref, w_ref, o_ref):
    o_ref[...] = x_ref[...] + jnp.dot(m_ref[...], w_ref[...], preferred_element_type=F32)


def _outproj(x2d, merged, w_out, tm):
    m = x2d.shape[0]
    tok = pl.BlockSpec((tm, D_MODEL), lambda i: (i, 0))
    return pl.pallas_call(
        _outproj_kernel,
        grid=(m // tm,),
        in_specs=[tok, tok, pl.BlockSpec((D_MODEL, D_MODEL), lambda i: (0, 0))],
        out_specs=tok,
        out_shape=jax.ShapeDtypeStruct((m, D_MODEL), F32),
        compiler_params=_cparams(("parallel",)),
        name="outproj",
    )(x2d, merged, w_out)


def _rmsnorm_kernel(x_ref, g_ref, o_ref):
    x = x_ref[...]
    o_ref[...] = x * lax.rsqrt(jnp.mean(x * x, axis=-1, keepdims=True) + NORM_EPS) * g_ref[...]


def _rmsnorm(x2d, g, tm):
    m = x2d.shape[0]
    tok = pl.BlockSpec((tm, D_MODEL), lambda i: (i, 0))
    return pl.pallas_call(
        _rmsnorm_kernel,
        grid=(m // tm,),
        in_specs=[tok, pl.BlockSpec((1, D_MODEL), lambda i: (0, 0))],
        out_specs=tok,
        out_shape=jax.ShapeDtypeStruct((m, D_MODEL), F32),
        compiler_params=_cparams(("parallel",)),
        name="final_norm",
    )(x2d, g.reshape(1, -1))


def _permute_w_in(wt):
    pad = jnp.zeros((LANE - H_C, D_MODEL), wt.dtype)
    return jnp.concatenate([wt[8332:14476], wt[1664:2176], wt[0:1664], wt[2176:7552],
                            wt[7564:8332], wt[7552:7564], pad], axis=0).astype(BF16)


def kernel(x_prompt, x_sample, cache_diff_k, cache_diff_v, cache_fox_k, cache_fox_v, cache_fox_logf,
           state_rwkv_wkv, state_rwkv_shift, page_table, norm_g, w_in, rwkv_mu, rwkv_w0, rwkv_w2,
           rwkv_a0, rwkv_a2, rwkv_k_k, rwkv_k_a, rwkv_r_k, rwkv_gn_g, rwkv_gn_b, diff_lq1, diff_lk1,
           diff_lq2, diff_lk2, diff_subln_g, fox_b_f, w_branch, w_out, final_g):
    bsz, t, _ = x_prompt.shape
    n = x_sample.shape[0]
    mp = bsz * t
    hp = x_prompt.reshape(mp, D_MODEL)
    hs = x_sample.reshape(n, D_MODEL)
    page_flat = page_table.reshape(-1).astype(jnp.int32)
    bd = jnp.kron(jnp.eye(H_A, dtype=F32), jnp.ones((N_A, N_A), F32))
    n_pool = cache_fox_logf.shape[1]
    assert n_pool % 8 == 0 and n == LANE
    dk_all = jnp.transpose(cache_diff_k, (0, 1, 3, 2, 4))
    dv_all = jnp.transpose(cache_diff_v, (0, 1, 3, 2, 4))
    fk_all = jnp.transpose(cache_fox_k, (0, 1, 3, 4, 2))
    fv_all = jnp.transpose(cache_fox_v, (0, 1, 3, 4, 2))
    lft_all = jnp.transpose(cache_fox_logf, (0, 3, 1, 2)).reshape(DEPTH, H_C, n_pool // 8, 8, PAGE)
    state_t = jnp.transpose(state_rwkv_wkv, (0, 2, 3, 4, 1))
    zero_lora = jnp.zeros((64, W_A), BF16)
    w_in_t = jnp.transpose(w_in, (2, 0, 1))
    outs_p, outs_s = [], []
    for l in range(DEPTH):
        lam_init = 0.8 - 0.6 * math.exp(-0.3 * l)
        w_p = _permute_w_in(w_in_t[:, l, :])
        w2p = jnp.concatenate([rwkv_w2[l].astype(BF16), zero_lora], axis=0)
        a2p = jnp.concatenate([zero_lora, rwkv_a2[l].astype(BF16)], axis=0)
        bf_pad = jnp.pad(fox_b_f[l], (0, LANE - H_C)).reshape(1, LANE)
        w_br = w_branch[l].astype(BF16)
        w_o = w_out[l].astype(BF16)
        rk = rwkv_r_k[l].reshape(-1)

        proj = _inproj(hp, norm_g[l], w_p, tm=1024)
        r, dec, k2, v, kk, bb, bonus = _rwkv_prep(
            proj, None, rwkv_mu[l], rwkv_w0[l], rwkv_a0[l], rwkv_k_k[l],
            rwkv_k_a[l], rk, w2p, a2p, bd, tm=512, seq_len=t)
        y_t, s_fin = _rwkv_scan(
            _to_scan_tiles(dec, bsz, t), _to_scan_tiles(kk, bsz, t), _to_scan_tiles(bb, bsz, t),
            _to_scan_tiles(k2, bsz, t), _to_scan_tiles(r, bsz, t), _to_scan_v(v, bsz, t),
            jnp.zeros((V_HI, N_A, LANE), F32))
        y_raw = _from_scan_y(y_t, bsz, t)
        yb, dk_new, dv_new = _diff_attn(proj, bsz, t, diff_subln_g[l], diff_lq1[l], diff_lk1[l],
                                        diff_lq2[l], diff_lk2[l], lam_init)
        _, cum, cumt, lft = _foxcum(proj, bsz, t, bf_pad, tm=256)
        yc, fk_new, fv_new = _fox_attn(proj, cum, cumt, bsz, t)
        merged = _merge(proj, y_raw, bonus, rwkv_gn_g[l], rwkv_gn_b[l], bd, yb, yc, w_br, tm=256)
        hp = _outproj(hp, merged, w_o, tm=512)
        outs_p.append((
            jnp.transpose(dk_new, (0, 2, 1, 3)),
            jnp.transpose(dv_new, (0, 2, 1, 3)),
            jnp.transpose(fk_new.reshape(bsz, H_C, D_C, t), (0, 3, 1, 2)),
            jnp.transpose(fv_new.reshape(bsz, H_C, D_C, t), (0, 3, 1, 2)),
            jnp.transpose(lft[:, :H_C, :], (0, 2, 1)),
            _from_scan_state(s_fin, bsz),
            proj.reshape(bsz, t, NP)[:, t - 1, C_U:C_U + A_SHIFT_W]))

        projs = _inproj(hs, norm_g[l], w_p, tm=n)
        us = projs[:, C_U:C_U + A_SHIFT_W]
        r, dec, k2, v, kk, bb, bonus = _rwkv_prep(
            projs, state_rwkv_shift[l], rwkv_mu[l], rwkv_w0[l], rwkv_a0[l], rwkv_k_k[l],
            rwkv_k_a[l], rk, w2p, a2p, bd, tm=n)
        s_new, y_t = _rwkv_step(state_t, l, dec, kk, bb, k2, r, v)
        y_raw = y_t.T
        lfs, _, _, _ = _foxcum(projs, 1, n, bf_pad, tm=n)
        q_b = projs[:, C_BQ:C_BQ + W_B]
        k_b = projs[:, C_BK:C_BK + W_B]
        v_b = projs[:, C_BV:C_BV + W_B]
        g_b = projs[:, C_BG:C_BG + W_B]
        yb = _diff_decode(page_flat, q_b, k_b, v_b, g_b, dk_all, dv_all, l,
                          diff_subln_g[l], diff_lq1[l], diff_lk1[l], diff_lq2[l], diff_lk2[l], lam_init)
        q_c = projs[:, C_CQ:C_CQ + W_C]
        k_c = projs[:, C_CK:C_CK + W_C]
        v_c = projs[:, C_CV:C_CV + W_C]
        g_c = projs[:, C_CG:C_CG + W_C]
        lfn_rep = jnp.broadcast_to(lfs[:, :16, None], (n, 16, LANE))
        yc = _fox_decode(page_flat, q_c, k_c, v_c, g_c, lfn_rep, fk_all, fv_all, lft_all, l)
        merged = _merge(projs, y_raw, bonus, rwkv_gn_g[l], rwkv_gn_b[l], bd, yb, yc, w_br, tm=n)
        hs = _outproj(hs, merged, w_o, tm=n)
        outs_s.append((
            k_b.reshape(n, 1, H_B, 2 * D_B), v_b.reshape(n, 1, H_B, 2 * D_B),
            k_c.reshape(n, 1, H_C, D_C), v_c.reshape(n, 1, H_C, D_C),
            lfs[:, :H_C].reshape(n, 1, H_C),
            jnp.transpose(s_new, (3, 0, 1, 2)),
            us))

    y_prompt = _rmsnorm(hp, final_g, tm=512).reshape(bsz, t, D_MODEL)
    y_sample = _rmsnorm(hs, final_g, tm=n).reshape(n, 1, D_MODEL)
    st_p = [jnp.stack([o[i] for o in outs_p]) for i in range(7)]
    st_s = [jnp.stack([o[i] for o in outs_s]) for i in range(7)]
    return (y_prompt, y_sample, st_p[0], st_s[0], st_p[1], st_s[1], st_p[2], st_s[2], st_p[3], st_s[3],
            st_p[4], st_s[4], st_p[5], st_s[5], st_p[6], st_s[6])
```

```python
import functools
import math

import jax
import jax.numpy as jnp
from jax import lax
from jax.experimental import pallas as pl
from jax.experimental.pallas import tpu as pltpu

F32 = jnp.float32
BF16 = jnp.bfloat16
HI = lax.Precision.HIGHEST

D_MODEL = 2048
DEPTH = 2
PAGE = 128
H_A, N_A = 8, 64
W_A = H_A * N_A
A_SHIFT_W = 3 * W_A + 128
H_B, D_B = 6, 64
W_B = H_B * 2 * D_B
H_C, D_C = 12, 64
W_C = H_C * D_C
NORM_EPS = 1e-6
SUBLN_EPS = 1e-5
GN_EPS = 64e-5
LANE = 128

C_GATES = 0
C_AGATE = 6144
C_U = 6656
C_BQ, C_BK, C_BV, C_BG = 8320, 9088, 9856, 10624
C_CQ, C_CK, C_CV, C_CG = 11392, 12160, 12928, 13696
C_CF = 14464
NP = 14592
TN_IN = 768

NEG = -1e30


def _cparams(sem, vmem_mb=48):
    return pltpu.CompilerParams(dimension_semantics=sem, vmem_limit_bytes=vmem_mb * 1024 * 1024)


def _softplus(z):
    return jnp.maximum(z, 0.0) + jnp.log1p(jnp.exp(-jnp.abs(z)))


def _sigmoid(z):
    return jax.nn.sigmoid(z)


def _silu(z):
    return z * jax.nn.sigmoid(z)


def _dot_nt(a, b):
    return lax.dot_general(a, b, (((1,), (1,)), ((), ())), preferred_element_type=F32)


def _inproj_kernel(x_ref, g_ref, w_ref, o_ref, h_ref):
    @pl.when(pl.program_id(1) == 0)
    def _():
        x = x_ref[...]
        ms = jnp.mean(x * x, axis=-1, keepdims=True)
        h_ref[...] = (x * lax.rsqrt(ms + NORM_EPS) * g_ref[...]).astype(BF16)

    o_ref[...] = _dot_nt(h_ref[...], w_ref[...])


def _inproj(x2d, g, w_p, tm):
    m = x2d.shape[0]
    return pl.pallas_call(
        _inproj_kernel,
        grid=(m // tm, NP // TN_IN),
        in_specs=[pl.BlockSpec((tm, D_MODEL), lambda i, j: (i, 0)),
                  pl.BlockSpec((1, D_MODEL), lambda i, j: (0, 0)),
                  pl.BlockSpec((TN_IN, D_MODEL), lambda i, j: (j, 0))],
        out_specs=pl.BlockSpec((tm, TN_IN), lambda i, j: (i, j)),
        out_shape=jax.ShapeDtypeStruct((m, NP), F32),
        scratch_shapes=[pltpu.VMEM((tm, D_MODEL), BF16)],
        compiler_params=_cparams(("parallel", "arbitrary")),
        name="inproj",
    )(x2d, g.reshape(1, D_MODEL), w_p)


def _rwkv_prep_kernel(u_ref, prev_ref, mu_ref, w0_ref, a0_ref, kk_ref, ka_ref, rk_ref,
                      w2_ref, a2_ref, bd_ref,
                      r_o, w_o, k_o, v_o, kk_o, b_o, bonus_o, *, blocks_per_seq):
    u = u_ref[...]
    if blocks_per_seq is None:
        prev = prev_ref[...]
    else:
        tm = u.shape[0]
        carry_in = jnp.where(pl.program_id(0) % blocks_per_seq == 0, 0.0, prev_ref[7:8, :])
        row = lax.broadcasted_iota(jnp.int32, (tm, 1), 0)
        prev = jnp.where(row == 0, carry_in, pltpu.roll(u, 1, axis=0))
    m = u + mu_ref[...] * (prev - u)
    r = m[:, 0:W_A]
    k = m[:, W_A:2 * W_A]
    v = m[:, 2 * W_A:3 * W_A]
    lo = m[:, 3 * W_A:3 * W_A + 128]
    zw = w0_ref[...] + jnp.dot(jnp.tanh(lo).astype(BF16), w2_ref[...], preferred_element_type=F32)
    w_log = -_softplus(-zw) - 0.5
    a = _sigmoid(a0_ref[...] + jnp.dot(lo.astype(BF16), a2_ref[...], preferred_element_type=F32))
    decay = jnp.exp(-jnp.exp(w_log))
    bd = bd_ref[...]
    kk = k * kk_ref[...]
    ss = jnp.dot(kk * kk, bd, precision=HI, preferred_element_type=F32)
    kk = kk * lax.rsqrt(jnp.maximum(ss, 1e-24))
    k2 = k * (1.0 + (a - 1.0) * ka_ref[...])
    bonus = jnp.dot(r * k2 * rk_ref[...], bd, precision=HI, preferred_element_type=F32) * v
    r_o[...] = r
    w_o[...] = decay
    k_o[...] = k2
    v_o[...] = v
    kk_o[...] = kk
    b_o[...] = kk * a
    bonus_o[...] = bonus


def _rwkv_prep(proj, prev, mu, w0, a0, k_k, k_a, r_k, w2p, a2p, bd, tm, seq_len=None):
    m = proj.shape[0]
    row = lambda n: pl.BlockSpec((1, n), lambda i: (0, 0))
    full = lambda a: pl.BlockSpec(a.shape, lambda i: (0, 0))
    tok = pl.BlockSpec((tm, W_A), lambda i: (i, 0))
    if prev is None:
        blocks_per_seq = seq_len // tm
        prev = proj
        prev_spec = pl.BlockSpec((8, A_SHIFT_W),
                                 lambda i: (jnp.maximum(i * (tm // 8) - 1, 0), C_U // A_SHIFT_W))
    else:
        blocks_per_seq = None
        prev_spec = pl.BlockSpec((tm, A_SHIFT_W), lambda i: (i, 0))
    return pl.pallas_call(
        functools.partial(_rwkv_prep_kernel, blocks_per_seq=blocks_per_seq),
        grid=(m // tm,),
        in_specs=[pl.BlockSpec((tm, A_SHIFT_W), lambda i: (i, C_U // A_SHIFT_W)),
                  prev_spec,
                  row(A_SHIFT_W), row(W_A), row(W_A), row(W_A), row(W_A), row(W_A),
                  full(w2p), full(a2p), full(bd)],
        out_specs=[tok] * 7,
        out_shape=[jax.ShapeDtypeStruct((m, W_A), F32)] * 7,
        compiler_params=_cparams(("parallel",)),
        name="rwkv_prep",
    )(proj, prev, mu.reshape(1, -1), w0.reshape(1, -1), a0.reshape(1, -1), k_k.reshape(1, -1),
      k_a.reshape(1, -1), r_k.reshape(1, -1), w2p, a2p, bd)


V_HI = 16
SCAN_TC = 16


def _rwkv_scan_kernel(w_ref, kk_ref, b_ref, k_ref, r_ref, v_ref, s0_ref, y_ref, s_ref):
    @pl.when(pl.program_id(0) == 0)
    def _():
        s_ref[...] = s0_ref[...]

    def step(t, carry):
        w = w_ref[t]
        kk = kk_ref[t]
        b = b_ref[t]
        kv = k_ref[t]
        r = r_ref[t]
        vt = v_ref[t]
        rows = []
        for vh in range(V_HI):
            s = s_ref[vh]
            sk = jnp.sum(s * kk, axis=0, keepdims=True)
            s2 = s * w - sk * b + vt[vh:vh + 1, :] * kv
            s_ref[vh] = s2
            rows.append(jnp.sum(s2 * r, axis=0, keepdims=True))
        y_ref[t] = jnp.concatenate(rows, axis=0)
        return carry

    lax.fori_loop(0, SCAN_TC, step, 0)


def _rwkv_scan(w_t, kk_t, b_t, k_t, r_t, v_t, s0):
    t = w_t.shape[0]
    tile = pl.BlockSpec((SCAN_TC, N_A, LANE), lambda i: (i, 0, 0))
    vspec = pl.BlockSpec((SCAN_TC, V_HI, LANE), lambda i: (i, 0, 0))
    sspec = pl.BlockSpec((V_HI, N_A, LANE), lambda i: (0, 0, 0))
    return pl.pallas_call(
        _rwkv_scan_kernel,
        grid=(t // SCAN_TC,),
        in_specs=[tile] * 5 + [vspec, sspec],
        out_specs=[vspec, sspec],
        out_shape=[jax.ShapeDtypeStruct((t, V_HI, LANE), F32),
                   jax.ShapeDtypeStruct((V_HI, N_A, LANE), F32)],
        compiler_params=_cparams(("arbitrary",)),
        name="rwkv_scan",
    )(w_t, kk_t, b_t, k_t, r_t, v_t, s0)


RT_C = 128


def _retile_kernel(*refs, bsz, is_v):
    ins = refs[:bsz]
    o_ref, z_ref = refs[bsz:]
    for b in range(bsz):
        xt = ins[b][...].T
        for h in range(H_A):
            z_ref[b * H_A + h] = xt[h * N_A:(h + 1) * N_A, :]
    if is_v:
        for vh in range(V_HI):
            rows = [z_ref[:, vh * (N_A // V_HI) + vl, :] for vl in range(N_A // V_HI)]
            o_ref[:, vh, :] = jnp.concatenate(rows, axis=0).T
    else:
        reps = LANE // (bsz * H_A)
        for c in range(N_A):
            o_ref[:, c, :] = jnp.concatenate([z_ref[:, c, :]] * reps, axis=0).T


def _retile(x, bsz, t, is_v):
    nt = t // RT_C
    rows = V_HI if is_v else N_A
    return pl.pallas_call(
        functools.partial(_retile_kernel, bsz=bsz, is_v=is_v),
        grid=(nt,),
        in_specs=[pl.BlockSpec((RT_C, W_A), functools.partial(lambda i, b: (b * nt + i, 0), b=b))
                  for b in range(bsz)],
        out_specs=pl.BlockSpec((RT_C, rows, LANE), lambda i: (i, 0, 0)),
        out_shape=jax.ShapeDtypeStruct((t, rows, LANE), F32),
        scratch_shapes=[pltpu.VMEM((bsz * H_A, N_A, RT_C), F32)],
        compiler_params=_cparams(("parallel",)),
        name="rwkv_retile_v" if is_v else "rwkv_retile",
    )(*([x] * bsz))


def _to_scan_tiles(x, bsz, t):
    return _retile(x, bsz, t, False)


def _to_scan_v(v, bsz, t):
    return _retile(v, bsz, t, True)


def _from_scan_y(y, bsz, t):
    y = y.reshape(t, V_HI, N_A // V_HI, bsz, H_A).transpose(3, 0, 4, 1, 2)
    return y.reshape(bsz * t, W_A)


def _from_scan_state(s, bsz):
    s = s.reshape(V_HI, N_A, N_A // V_HI, bsz, H_A).transpose(3, 4, 0, 2, 1)
    return s.reshape(bsz, H_A, N_A, N_A)


def _rwkv_step_kernel(s_ref, w_ref, kk_ref, b_ref, k_ref, r_ref, v_ref, so_ref, y_ref):
    w = w_ref[...]
    kk = kk_ref[...]
    b = b_ref[...]
    kv = k_ref[...]
    r = r_ref[...]

    def body(v, carry):
        s = s_ref[v]
        sk = jnp.sum(s * kk, axis=0, keepdims=True)
        s2 = s * w - sk * b + v_ref[pl.ds(v, 1), :] * kv
        so_ref[v] = s2
        y_ref[pl.ds(v, 1), :] = jnp.sum(s2 * r, axis=0, keepdims=True)
        return carry

    lax.fori_loop(0, N_A, body, 0, unroll=4)


def _rwkv_step(state_t, layer, w, kk, b, k2, r, v):
    n = w.shape[0]
    tile = pl.BlockSpec((N_A, n), lambda h: (h, 0))
    return pl.pallas_call(
        _rwkv_step_kernel,
        grid=(H_A,),
        in_specs=[pl.BlockSpec((None, None, N_A, N_A, n), lambda h: (layer, h, 0, 0, 0)),
                  tile, tile, tile, tile, tile, tile],
        out_specs=[pl.BlockSpec((None, N_A, N_A, n), lambda h: (h, 0, 0, 0)), tile],
        out_shape=[jax.ShapeDtypeStruct((H_A, N_A, N_A, n), F32),
                   jax.ShapeDtypeStruct((W_A, n), F32)],
        compiler_params=_cparams(("parallel",)),
        name="rwkv_step",
    )(state_t, w.T, kk.T, b.T, k2.T, r.T, v.T)


def _lam(lq1, lk1, lq2, lk2, lam_init):
    return (jnp.exp(jnp.sum(lq1[...] * lk1[...], axis=-1, keepdims=True))
            - jnp.exp(jnp.sum(lq2[...] * lk2[...], axis=-1, keepdims=True)) + lam_init)


def _online(s, m, l):
    m_new = jnp.maximum(m, jnp.max(s, axis=-1, keepdims=True))
    alpha = jnp.exp(m - m_new)
    p = jnp.exp(s - m_new)
    return p, alpha, m_new, l * alpha + jnp.sum(p, axis=-1, keepdims=True)


def _diff_attn_kernel(q_ref, k_ref, v_ref, g_ref, sg_ref, lq1, lk1, lq2, lk2, o_ref, ko_ref, vo_ref,
                      *, lam_init, tq, tk):
    qi = pl.program_id(2)

    @pl.when(qi == 0)
    def _():
        ko_ref[...] = k_ref[...]
        vo_ref[...] = v_ref[...]

    lane = lax.broadcasted_iota(jnp.int32, (1, LANE), 1)
    q = q_ref[...] * (D_B ** -0.5)
    q1 = jnp.where(lane < D_B, q, 0.0).astype(BF16)
    q2 = jnp.where(lane >= D_B, q, 0.0).astype(BF16)
    rows = qi * tq + lax.broadcasted_iota(jnp.int32, (tq, tk), 0)
    cols0 = lax.broadcasted_iota(jnp.int32, (tq, tk), 1)

    def body(j, carry):
        m1, l1, a1, m2, l2, a2 = carry
        off = pl.multiple_of(j * tk, tk)
        kb = k_ref[pl.ds(off, tk), :].astype(BF16)
        vb = v_ref[pl.ds(off, tk), :].astype(BF16)
        mask = (cols0 + off) <= rows
        s1 = jnp.where(mask, _dot_nt(q1, kb), NEG)
        s2 = jnp.where(mask, _dot_nt(q2, kb), NEG)
        p1, al1, m1, l1 = _online(s1, m1, l1)
        p2, al2, m2, l2 = _online(s2, m2, l2)
        a1 = a1 * al1 + jnp.dot(p1.astype(BF16), vb, preferred_element_type=F32)
        a2 = a2 * al2 + jnp.dot(p2.astype(BF16), vb, preferred_element_type=F32)
        return m1, l1, a1, m2, l2, a2

    mi = jnp.full((tq, 1), NEG, F32)
    zi = jnp.zeros((tq, 1), F32)
    ai = jnp.zeros((tq, LANE), F32)
    nkv = (qi * tq + tq + tk - 1) // tk
    m1, l1, a1, m2, l2, a2 = lax.fori_loop(0, nkv, body, (mi, zi, ai, mi, zi, ai))
    lam = _lam(lq1, lk1, lq2, lk2, lam_init)
    o = a1 / l1 - lam * (a2 / l2)
    o = o * lax.rsqrt(jnp.mean(o * o, axis=-1, keepdims=True) + SUBLN_EPS) * sg_ref[...]
    o_ref[...] = o * (1.0 - lam_init) * _silu(g_ref[...])


def _diff_attn(proj, bsz, t, subln_g, lq1, lk1, lq2, lk2, lam_init, tq=512, tk=512):
    nq = t // tq
    small = pl.BlockSpec((1, D_B), lambda b, h, i: (0, 0))
    return pl.pallas_call(
        functools.partial(_diff_attn_kernel, lam_init=lam_init, tq=tq, tk=tk),
        grid=(bsz, H_B, nq),
        in_specs=[pl.BlockSpec((tq, LANE), lambda b, h, i: (b * nq + i, C_BQ // LANE + h)),
                  pl.BlockSpec((t, LANE), lambda b, h, i: (b, C_BK // LANE + h)),
                  pl.BlockSpec((t, LANE), lambda b, h, i: (b, C_BV // LANE + h)),
                  pl.BlockSpec((tq, LANE), lambda b, h, i: (b * nq + i, C_BG // LANE + h)),
                  pl.BlockSpec((1, LANE), lambda b, h, i: (0, 0)),
                  small, small, small, small],
        out_specs=[pl.BlockSpec((tq, LANE), lambda b, h, i: (b * nq + i, h)),
                   pl.BlockSpec((None, None, t, LANE), lambda b, h, i: (b, h, 0, 0)),
                   pl.BlockSpec((None, None, t, LANE), lambda b, h, i: (b, h, 0, 0))],
        out_shape=[jax.ShapeDtypeStruct((bsz * t, W_B), F32),
                   jax.ShapeDtypeStruct((bsz, H_B, t, LANE), F32),
                   jax.ShapeDtypeStruct((bsz, H_B, t, LANE), F32)],
        compiler_params=_cparams(("parallel", "parallel", "arbitrary")),
        name="diff_attn",
    )(proj, proj, proj, proj, subln_g.reshape(1, -1), lq1.reshape(1, -1), lk1.reshape(1, -1),
      lq2.reshape(1, -1), lk2.reshape(1, -1))


def _foxcum_kernel(cf_ref, bf_ref, tri_ref, lf_ref, cum_ref, cumt_ref, lft_ref, carry_ref, *, tm):
    @pl.when(pl.program_id(1) == 0)
    def _():
        carry_ref[...] = jnp.zeros_like(carry_ref)

    lf = -_softplus(-(cf_ref[...] + bf_ref[...]))
    lf_ref[...] = lf
    lft_ref[...] = lf.T[:16, :]
    c = jnp.dot(tri_ref[...], lf, precision=HI, preferred_element_type=F32) + carry_ref[...]
    cum_ref[...] = c
    carry_ref[...] = c[tm - 1:tm, :]
    cumt_ref[...] = c.T[:16, :]


def _foxcum(proj, bsz, t, bf_pad, tm):
    nt = t // tm
    tri = (jnp.arange(tm)[:, None] >= jnp.arange(tm)[None, :]).astype(F32)
    tok = pl.BlockSpec((tm, LANE), lambda b, i: (b * nt + i, 0))
    return pl.pallas_call(
        functools.partial(_foxcum_kernel, tm=tm),
        grid=(bsz, nt),
        in_specs=[pl.BlockSpec((tm, LANE), lambda b, i: (b * nt + i, C_CF // LANE)),
                  pl.BlockSpec((1, LANE), lambda b, i: (0, 0)),
                  pl.BlockSpec((tm, tm), lambda b, i: (0, 0))],
        out_specs=[tok, tok, pl.BlockSpec((None, 16, tm), lambda b, i: (b, 0, i)),
                   pl.BlockSpec((None, 16, tm), lambda b, i: (b, 0, i))],
        out_shape=[jax.ShapeDtypeStruct((bsz * t, LANE), F32),
                   jax.ShapeDtypeStruct((bsz * t, LANE), F32),
                   jax.ShapeDtypeStruct((bsz, 16, t), F32),
                   jax.ShapeDtypeStruct((bsz, 16, t), F32)],
        scratch_shapes=[pltpu.VMEM((1, LANE), F32)],
        compiler_params=_cparams(("parallel", "arbitrary")),
        name="fox_cum",
    )(proj, bf_pad, tri)


def _fox_attn_kernel(q_ref, k_ref, v_ref, g_ref, cum_ref, cumt_ref, o_ref, kto_ref, vto_ref, *, tq, tk):
    hp = pl.program_id(1)
    qi = pl.program_id(2)

    @pl.when(qi == 0)
    def _():
        kto_ref[...] = k_ref[...].T
        vto_ref[...] = v_ref[...].T

    lane = lax.broadcasted_iota(jnp.int32, (1, LANE), 1)
    q = q_ref[...] * (D_C ** -0.5)
    qa = jnp.where(lane < D_C, q, 0.0).astype(BF16)
    qb = jnp.where(lane >= D_C, q, 0.0).astype(BF16)
    cq = cum_ref[...]
    cqa = jnp.sum(jnp.where(lane == 2 * hp, cq, 0.0), axis=-1, keepdims=True)
    cqb = jnp.sum(jnp.where(lane == 2 * hp + 1, cq, 0.0), axis=-1, keepdims=True)
    rows = qi * tq + lax.broadcasted_iota(jnp.int32, (tq, tk), 0)
    cols0 = lax.broadcasted_iota(jnp.int32, (tq, tk), 1)

    def body(j, carry):
        ma, la, aa, mb, lb, ab = carry
        off = pl.multiple_of(j * tk, tk)
        kb = k_ref[pl.ds(off, tk), :].astype(BF16)
        vb = v_ref[pl.ds(off, tk), :].astype(BF16)
        cka = cumt_ref[pl.ds(2 * hp, 1), pl.ds(off, tk)]
        ckb = cumt_ref[pl.ds(2 * hp + 1, 1), pl.ds(off, tk)]
        mask = (cols0 + off) <= rows
        sa = jnp.where(mask, _dot_nt(qa, kb) + (cqa - cka), NEG)
        sb = jnp.where(mask, _dot_nt(qb, kb) + (cqb - ckb), NEG)
        pa, ala, ma, la = _online(sa, ma, la)
        pb, alb, mb, lb = _online(sb, mb, lb)
        aa = aa * ala + jnp.dot(pa.astype(BF16), vb, preferred_element_type=F32)
        ab = ab * alb + jnp.dot(pb.astype(BF16), vb, preferred_element_type=F32)
        return ma, la, aa, mb, lb, ab

    mi = jnp.full((tq, 1), NEG, F32)
    zi = jnp.zeros((tq, 1), F32)
    ai = jnp.zeros((tq, LANE), F32)
    nkv = (qi * tq + tq + tk - 1) // tk
    ma, la, aa, mb, lb, ab = lax.fori_loop(0, nkv, body, (mi, zi, ai, mi, zi, ai))
    o = jnp.where(lane < D_C, aa / la, ab / lb)
    o_ref[...] = o * _silu(g_ref[...])


def _fox_attn(proj, cum, cumt, bsz, t, tq=512, tk=512):
    nq = t // tq
    return pl.pallas_call(
        functools.partial(_fox_attn_kernel, tq=tq, tk=tk),
        grid=(bsz, H_C // 2, nq),
        in_specs=[pl.BlockSpec((tq, LANE), lambda b, h, i: (b * nq + i, C_CQ // LANE + h)),
                  pl.BlockSpec((t, LANE), lambda b, h, i: (b, C_CK // LANE + h)),
                  pl.BlockSpec((t, LANE), lambda b, h, i: (b, C_CV // LANE + h)),
                  pl.BlockSpec((tq, LANE), lambda b, h, i: (b * nq + i, C_CG // LANE + h)),
                  pl.BlockSpec((tq, LANE), lambda b, h, i: (b * nq + i, 0)),
                  pl.BlockSpec((None, 16, t), lambda b, h, i: (b, 0, 0))],
        out_specs=[pl.BlockSpec((tq, LANE), lambda b, h, i: (b * nq + i, h)),
                   pl.BlockSpec((None, None, LANE, t), lambda b, h, i: (b, h, 0, 0)),
                   pl.BlockSpec((None, None, LANE, t), lambda b, h, i: (b, h, 0, 0))],
        out_shape=[jax.ShapeDtypeStruct((bsz * t, W_C), F32),
                   jax.ShapeDtypeStruct((bsz, H_C // 2, LANE, t), F32),
                   jax.ShapeDtypeStruct((bsz, H_C // 2, LANE, t), F32)],
        compiler_params=_cparams(("parallel", "parallel", "arbitrary")),
        name="fox_attn",
    )(proj, proj, proj, proj, cum, cumt)


PP = 8
N_PAGES = 16
NSTEP = N_PAGES // PP


def _diff_dec_kernel(pt_ref, q_ref, kn_ref, vn_ref, g_ref, sg_ref, lq1, lk1, lq2, lk2, *rest,
                     lam_init):
    kp = rest[:PP]
    vp = rest[PP:2 * PP]
    o_ref, qrep_ref, m_ref, l_ref, acc_ref = rest[2 * PP:]
    s = pl.program_id(1)
    lo = lax.broadcasted_iota(jnp.int32, (1, LANE), 1) < D_B

    def spread(x):
        sw = pltpu.roll(x, D_B, axis=1)
        return jnp.where(lo, x, sw), jnp.where(lo, sw, x)

    @pl.when(s == 0)
    def _():
        qrow = q_ref[...] * (D_B ** -0.5)
        r = lax.broadcasted_iota(jnp.int32, (LANE, LANE), 0)
        c = lax.broadcasted_iota(jnp.int32, (LANE, LANE), 1)
        for h in range(H_B):
            qh = qrow[:, h * LANE:(h + 1) * LANE]
            col = jnp.sum(jnp.where(r == c, qh, 0.0), axis=1, keepdims=True)
            same_map = (r < D_B) == (c < D_B)
            qrep_ref[h] = jnp.where(same_map, jnp.broadcast_to(col, (LANE, LANE)), 0.0).astype(BF16)
        m_ref[...] = jnp.full_like(m_ref, NEG)
        l_ref[...] = jnp.zeros_like(l_ref)
        acc_ref[...] = jnp.zeros_like(acc_ref)

    for h in range(H_B):
        w = qrep_ref[h]
        ss = [jnp.dot(kp[j][h].astype(BF16), w, preferred_element_type=F32) for j in range(PP)]
        mx = ss[0]
        for j in range(1, PP):
            mx = jnp.maximum(mx, ss[j])
        m_old = m_ref[h][0:1, :]
        m_new = jnp.maximum(m_old, jnp.max(mx, axis=0, keepdims=True))
        alpha = jnp.exp(m_old - m_new)
        al1, al2 = spread(alpha)
        acc1 = acc_ref[2 * h] * al1
        acc2 = acc_ref[2 * h + 1] * al2
        l = l_ref[h] * alpha
        for j in range(PP):
            p = jnp.exp(ss[j] - m_new)
            p1, p2 = spread(p)
            v = vp[j][h]
            acc1 = acc1 + p1 * v
            acc2 = acc2 + p2 * v
            l = l + p
        acc_ref[2 * h] = acc1
        acc_ref[2 * h + 1] = acc2
        l_ref[h] = l
        m_ref[h] = jnp.broadcast_to(m_new, (8, LANE))

    @pl.when(s == NSTEP - 1)
    def _():
        lam = _lam(lq1, lk1, lq2, lk2, lam_init)
        for h in range(H_B):
            knh = jnp.broadcast_to(kn_ref[:, h * LANE:(h + 1) * LANE], (8, LANE)).astype(BF16)
            s_new = jnp.dot(knh, qrep_ref[h], preferred_element_type=F32)[0:1, :]
            vnh = vn_ref[:, h * LANE:(h + 1) * LANE]
            m_old = m_ref[h][0:1, :]
            m_f = jnp.maximum(m_old, s_new)
            alpha = jnp.exp(m_old - m_f)
            pn = jnp.exp(s_new - m_f)
            den = jnp.sum(l_ref[h], axis=0, keepdims=True) * alpha + pn
            als, pns, dens = spread(alpha), spread(pn), spread(den)
            outs = [(jnp.sum(acc_ref[2 * h + mp], axis=0, keepdims=True) * als[mp] + pns[mp] * vnh)
                    / dens[mp] for mp in range(2)]
            od = outs[0] - lam * outs[1]
            od = od * lax.rsqrt(jnp.mean(od * od, axis=-1, keepdims=True) + SUBLN_EPS) * sg_ref[...]
            gate = g_ref[:, h * LANE:(h + 1) * LANE]
            o_ref[:, h * LANE:(h + 1) * LANE] = od * (1.0 - lam_init) * _silu(gate)


def _diff_decode(page_flat, q, kn, vn, g, cache_k, cache_v, layer, subln_g, lq1, lk1, lq2, lk2,
                 lam_init):
    n = q.shape[0]
    full = pl.BlockSpec((None, 1, W_B), lambda b, s, pt: (b, 0, 0))
    small = pl.BlockSpec((1, D_B), lambda b, s, pt: (0, 0))
    seq = lambda x: x.reshape(n, 1, W_B)

    def page_spec(j):
        return pl.BlockSpec((None, None, H_B, PAGE, 2 * D_B),
                            lambda b, s, pt: (layer, pt[b * N_PAGES + s * PP + j], 0, 0, 0))

    grid_spec = pltpu.PrefetchScalarGridSpec(
        num_scalar_prefetch=1,
        grid=(n, NSTEP),
        in_specs=[full, full, full, full, pl.BlockSpec((1, LANE), lambda b, s, pt: (0, 0)),
                  small, small, small, small]
                 + [page_spec(j) for j in range(PP)] + [page_spec(j) for j in range(PP)],
        out_specs=full,
        scratch_shapes=[pltpu.VMEM((H_B, LANE, LANE), BF16),
                        pltpu.VMEM((H_B, 8, LANE), F32),
                        pltpu.VMEM((H_B, PAGE, LANE), F32),
                        pltpu.VMEM((2 * H_B, PAGE, LANE), F32)],
    )
    return pl.pallas_call(
        functools.partial(_diff_dec_kernel, lam_init=lam_init),
        grid_spec=grid_spec,
        out_shape=jax.ShapeDtypeStruct((n, 1, W_B), F32),
        compiler_params=_cparams(("arbitrary", "arbitrary")),
        name="diff_decode",
    )(page_flat, seq(q), seq(kn), seq(vn), seq(g), subln_g.reshape(1, -1), lq1.reshape(1, -1),
      lk1.reshape(1, -1), lq2.reshape(1, -1), lk2.reshape(1, -1),
      *([cache_k] * PP), *([cache_v] * PP)).reshape(n, W_B)


def _fox_dec_kernel(pt_ref, q_ref, kn_ref, vn_ref, g_ref, lfn_ref, tri_ref, *rest):
    kp = rest[:PP]
    vp = rest[PP:2 * PP]
    lfp = rest[2 * PP:3 * PP]
    o_ref, qcol_ref, m_ref, l_ref, c_ref, acc_ref = rest[3 * PP:]
    b = pl.program_id(0)
    s = pl.program_id(1)

    @pl.when(s == 0)
    def _():
        qrow = q_ref[...] * (D_C ** -0.5)
        r = lax.broadcasted_iota(jnp.int32, (D_C, D_C), 0)
        c = lax.broadcasted_iota(jnp.int32, (D_C, D_C), 1)
        for h in range(H_C):
            qh = qrow[:, h * D_C:(h + 1) * D_C]
            col = jnp.sum(jnp.where(r == c, qh, 0.0), axis=1, keepdims=True)
            qcol_ref[h] = jnp.broadcast_to(col, (D_C, PAGE))
        m_ref[...] = jnp.full_like(m_ref, NEG)
        l_ref[...] = jnp.zeros_like(l_ref)
        c_ref[...] = jnp.zeros_like(c_ref)
        acc_ref[...] = jnp.zeros_like(acc_ref)

    tri = tri_ref[...]
    carry = c_ref[...][:, :1]
    row16 = lax.broadcasted_iota(jnp.int32, (16, PAGE), 0)
    tiles = []
    for j in range(PP):
        sub = pt_ref[b * N_PAGES + s * PP + j] % 8
        sc_t = jnp.zeros((16, PAGE), F32)
        lf = jnp.zeros((16, PAGE), F32)
        for h in range(H_C):
            sh = jnp.sum(kp[j][h] * qcol_ref[h], axis=0, keepdims=True)
            sc_t = jnp.where(row16 == h, sh, sc_t)
            lf = jnp.where(row16 == h, lfp[j][h, pl.ds(sub, 1), :], lf)
        cum = jnp.dot(lf, tri, precision=HI, preferred_element_type=F32) + carry
        carry = carry + jnp.sum(lf, axis=-1, keepdims=True)
        tiles.append(sc_t - cum)
    sc = jnp.concatenate(tiles, axis=1)
    p, alpha, m_new, l_new = _online(sc, m_ref[...][:, :1], l_ref[...][:, :1])
    m_ref[...] = jnp.broadcast_to(m_new, (16, LANE))
    l_ref[...] = jnp.broadcast_to(l_new, (16, LANE))
    c_ref[...] = jnp.broadcast_to(carry, (16, LANE))
    for h in range(H_C):
        acc = acc_ref[h] * alpha[h:h + 1, :]
        for j in range(PP):
            acc = acc + vp[j][h] * p[h:h + 1, j * PAGE:(j + 1) * PAGE]
        acc_ref[h] = acc

    @pl.when(s == NSTEP - 1)
    def _():
        qrow = q_ref[...] * (D_C ** -0.5)
        rcol = lax.broadcasted_iota(jnp.int32, (16, 1), 0)
        s_new = jnp.zeros((16, 1), F32)
        for h in range(H_C):
            dot = jnp.sum(qrow[:, h * D_C:(h + 1) * D_C] * kn_ref[:, h * D_C:(h + 1) * D_C],
                          axis=-1, keepdims=True)
            s_new = jnp.where(rcol == h, dot, s_new)
        s_new = s_new - (carry + lfn_ref[...][:, :1])
        m_f = jnp.maximum(m_new, s_new)
        al = jnp.exp(m_new - m_f)
        pn = jnp.exp(s_new - m_f)
        l_f = l_new * al + pn
        ones = jnp.ones((8, PAGE), F32)
        outs = []
        for h in range(H_C):
            past = lax.dot_general(ones, acc_ref[h], (((1,), (1,)), ((), ())), precision=HI,
                                   preferred_element_type=F32)[0:1, :]
            vnh = vn_ref[:, h * D_C:(h + 1) * D_C]
            outs.append((past * al[h:h + 1, :] + pn[h:h + 1, :] * vnh) / l_f[h:h + 1, :])
        o_all = jnp.concatenate(outs, axis=1)
        o_ref[...] = o_all * _silu(g_ref[...])


def _fox_decode(page_flat, q, kn, vn, g, lfn_rep, cache_k, cache_v, cache_lft, layer):
    n = q.shape[0]
    full = pl.BlockSpec((None, 1, W_C), lambda b, s, pt: (b, 0, 0))
    seq = lambda x: x.reshape(n, 1, W_C)
    tri = (jnp.arange(PAGE)[:, None] <= jnp.arange(PAGE)[None, :]).astype(F32)

    def page_spec(j):
        return pl.BlockSpec((None, None, H_C, D_C, PAGE),
                            lambda b, s, pt: (layer, pt[b * N_PAGES + s * PP + j], 0, 0, 0))

    def lf_spec(j):
        return pl.BlockSpec((None, H_C, None, 8, PAGE),
                            lambda b, s, pt: (layer, 0, pt[b * N_PAGES + s * PP + j] // 8, 0, 0))

    grid_spec = pltpu.PrefetchScalarGridSpec(
        num_scalar_prefetch=1,
        grid=(n, NSTEP),
        in_specs=[full, full, full, full,
                  pl.BlockSpec((None, 16, LANE), lambda b, s, pt: (b, 0, 0)),
                  pl.BlockSpec((PAGE, PAGE), lambda b, s, pt: (0, 0))]
                 + [page_spec(j) for j in range(PP)] + [page_spec(j) for j in range(PP)]
                 + [lf_spec(j) for j in range(PP)],
        out_specs=full,
        scratch_shapes=[pltpu.VMEM((H_C, D_C, PAGE), F32)] + [pltpu.VMEM((16, LANE), F32)] * 3
                       + [pltpu.VMEM((H_C, D_C, PAGE), F32)],
    )
    return pl.pallas_call(
        _fox_dec_kernel,
        grid_spec=grid_spec,
        out_shape=jax.ShapeDtypeStruct((n, 1, W_C), F32),
        compiler_params=_cparams(("arbitrary", "arbitrary")),
        name="fox_decode",
    )(page_flat, seq(q), seq(kn), seq(vn), seq(g), lfn_rep, tri, *([cache_k] * PP),
      *([cache_v] * PP), *([cache_lft] * PP)).reshape(n, W_C)


def _merge_kernel(ga_ref, gb_ref, gc_ref, ag_ref, y_ref, bonus_ref, gng_ref, gnb_ref, bd_ref,
                  yb_ref, yc_ref, w_ref, o_ref):
    y = y_ref[...]
    bd = bd_ref[...]
    mean = jnp.dot(y, bd, precision=HI, preferred_element_type=F32) * (1.0 / N_A)
    d = y - mean
    var = jnp.dot(d * d, bd, precision=HI, preferred_element_type=F32) * (1.0 / N_A)
    yn = d * lax.rsqrt(var + GN_EPS) * gng_ref[...] + gnb_ref[...] + bonus_ref[...]
    ya = yn * _silu(ag_ref[...])
    pa = jnp.dot(ya.astype(BF16), w_ref[0:W_A, :], preferred_element_type=F32)
    pb = jnp.dot(yb_ref[...].astype(BF16), w_ref[W_A:W_A + W_B, :], preferred_element_type=F32)
    pc = jnp.dot(yc_ref[...].astype(BF16), w_ref[W_A + W_B:, :], preferred_element_type=F32)
    merged = _sigmoid(ga_ref[...]) * pa + _sigmoid(gb_ref[...]) * pb + _sigmoid(gc_ref[...]) * pc
    o_ref[...] = merged.astype(BF16)


def _merge(proj, y_raw, bonus, gn_g, gn_b, bd, yb, yc, w_br, tm):
    m = proj.shape[0]
    gate = lambda c: pl.BlockSpec((tm, D_MODEL), lambda i: (i, c))
    tok = lambda n: pl.BlockSpec((tm, n), lambda i: (i, 0))
    row = pl.BlockSpec((1, W_A), lambda i: (0, 0))
    return pl.pallas_call(
        _merge_kernel,
        grid=(m // tm,),
        in_specs=[gate(0), gate(1), gate(2),
                  pl.BlockSpec((tm, W_A), lambda i: (i, C_AGATE // W_A)),
                  tok(W_A), tok(W_A), row, row, pl.BlockSpec((W_A, W_A), lambda i: (0, 0)),
                  tok(W_B), tok(W_C), pl.BlockSpec((D_MODEL, D_MODEL), lambda i: (0, 0))],
        out_specs=tok(D_MODEL),
        out_shape=jax.ShapeDtypeStruct((m, D_MODEL), BF16),
        compiler_params=_cparams(("parallel",)),
        name="merge",
    )(proj, proj, proj, proj, y_raw, bonus, gn_g.reshape(1, -1), gn_b.reshape(1, -1), bd, yb, yc, w_br)


def _outproj_kernel(x_ref, m_ref, w_ref, o_ref):
    o_ref[...] = x_ref[...] + jnp.dot(m_ref[...], w_ref[...], preferred_element_type=F32)


def _outproj(x2d, merged, w_out, tm):
    m = x2d.shape[0]
    tok = pl.BlockSpec((tm, D_MODEL), lambda i: (i, 0))
    return pl.pallas_call(
        _outproj_kernel,
        grid=(m // tm,),
        in_specs=[tok, tok, pl.BlockSpec((D_MODEL, D_MODEL), lambda i: (0, 0))],
        out_specs=tok,
        out_shape=jax.ShapeDtypeStruct((m, D_MODEL), F32),
        compiler_params=_cparams(("parallel",)),
        name="outproj",
    )(x2d, merged, w_out)


def _rmsnorm_kernel(x_ref, g_ref, o_ref):
    x = x_ref[...]
    o_ref[...] = x * lax.rsqrt(jnp.mean(x * x, axis=-1, keepdims=True) + NORM_EPS) * g_ref[...]


def _rmsnorm(x2d, g, tm):
    m = x2d.shape[0]
    tok = pl.BlockSpec((tm, D_MODEL), lambda i: (i, 0))
    return pl.pallas_call(
        _rmsnorm_kernel,
        grid=(m // tm,),
        in_specs=[tok, pl.BlockSpec((1, D_MODEL), lambda i: (0, 0))],
        out_specs=tok,
        out_shape=jax.ShapeDtypeStruct((m, D_MODEL), F32),
        compiler_params=_cparams(("parallel",)),
        name="final_norm",
    )(x2d, g.reshape(1, -1))


def _permute_w_in(wt):
    pad = jnp.zeros((LANE - H_C, D_MODEL), wt.dtype)
    return jnp.concatenate([wt[8332:14476], wt[1664:2176], wt[0:1664], wt[2176:7552],
                            wt[7564:8332], wt[7552:7564], pad], axis=0).astype(BF16)


def kernel(x_prompt, x_sample, cache_diff_k, cache_diff_v, cache_fox_k, cache_fox_v, cache_fox_logf,
           state_rwkv_wkv, state_rwkv_shift, page_table, norm_g, w_in, rwkv_mu, rwkv_w0, rwkv_w2,
           rwkv_a0, rwkv_a2, rwkv_k_k, rwkv_k_a, rwkv_r_k, rwkv_gn_g, rwkv_gn_b, diff_lq1, diff_lk1,
           diff_lq2, diff_lk2, diff_subln_g, fox_b_f, w_branch, w_out, final_g):
    bsz, t, _ = x_prompt.shape
    n = x_sample.shape[0]
    mp = bsz * t
    hp = x_prompt.reshape(mp, D_MODEL)
    hs = x_sample.reshape(n, D_MODEL)
    page_flat = page_table.reshape(-1).astype(jnp.int32)
    bd = jnp.kron(jnp.eye(H_A, dtype=F32), jnp.ones((N_A, N_A), F32))
    n_pool = cache_fox_logf.shape[1]
    assert n_pool % 8 == 0 and n == LANE
    dk_all = jnp.transpose(cache_diff_k, (0, 1, 3, 2, 4))
    dv_all = jnp.transpose(cache_diff_v, (0, 1, 3, 2, 4))
    fk_all = jnp.transpose(cache_fox_k, (0, 1, 3, 4, 2))
    fv_all = jnp.transpose(cache_fox_v, (0, 1, 3, 4, 2))
    lft_all = jnp.transpose(cache_fox_logf, (0, 3, 1, 2)).reshape(DEPTH, H_C, n_pool // 8, 8, PAGE)
    state_t = jnp.transpose(state_rwkv_wkv, (0, 2, 3, 4, 1))
    zero_lora = jnp.zeros((64, W_A), BF16)
    w_in_t = jnp.transpose(w_in, (2, 0, 1)).astype(BF16)
    outs_p, outs_s = [], []
    for l in range(DEPTH):
        lam_init = 0.8 - 0.6 * math.exp(-0.3 * l)
        w_p = _permute_w_in(w_in_t[:, l, :])
        w2p = jnp.concatenate([rwkv_w2[l].astype(BF16), zero_lora], axis=0)
        a2p = jnp.concatenate([zero_lora, rwkv_a2[l].astype(BF16)], axis=0)
        bf_pad = jnp.pad(fox_b_f[l], (0, LANE - H_C)).reshape(1, LANE)
        w_br = w_branch[l].astype(BF16)
        w_o = w_out[l].astype(BF16)
        rk = rwkv_r_k[l].reshape(-1)

        proj = _inproj(hp, norm_g[l], w_p, tm=1024)
        r, dec, k2, v, kk, bb, bonus = _rwkv_prep(
            proj, None, rwkv_mu[l], rwkv_w0[l], rwkv_a0[l], rwkv_k_k[l],
            rwkv_k_a[l], rk, w2p, a2p, bd, tm=512, seq_len=t)
        y_t, s_fin = _rwkv_scan(
            _to_scan_tiles(dec, bsz, t), _to_scan_tiles(kk, bsz, t), _to_scan_tiles(bb, bsz, t),
            _to_scan_tiles(k2, bsz, t), _to_scan_tiles(r, bsz, t), _to_scan_v(v, bsz, t),
            jnp.zeros((V_HI, N_A, LANE), F32))
        y_raw = _from_scan_y(y_t, bsz, t)
        yb, dk_new, dv_new = _diff_attn(proj, bsz, t, diff_subln_g[l], diff_lq1[l], diff_lk1[l],
                                        diff_lq2[l], diff_lk2[l], lam_init)
        _, cum, cumt, lft = _foxcum(proj, bsz, t, bf_pad, tm=256)
        yc, fk_new, fv_new = _fox_attn(proj, cum, cumt, bsz, t)
        merged = _merge(proj, y_raw, bonus, rwkv_gn_g[l], rwkv_gn_b[l], bd, yb, yc, w_br, tm=256)
        hp = _outproj(hp, merged, w_o, tm=512)
        outs_p.append((
            jnp.transpose(dk_new, (0, 2, 1, 3)),
            jnp.transpose(dv_new, (0, 2, 1, 3)),
            jnp.transpose(fk_new.reshape(bsz, H_C, D_C, t), (0, 3, 1, 2)),
            jnp.transpose(fv_new.reshape(bsz, H_C, D_C, t), (0, 3, 1, 2)),
            jnp.transpose(lft[:, :H_C, :], (0, 2, 1)),
            _from_scan_state(s_fin, bsz),
            proj.reshape(bsz, t, NP)[:, t - 1, C_U:C_U + A_SHIFT_W]))

        projs = _inproj(hs, norm_g[l], w_p, tm=n)
        us = projs[:, C_U:C_U + A_SHIFT_W]
        r, dec, k2, v, kk, bb, bonus = _rwkv_prep(
            projs, state_rwkv_shift[l], rwkv_mu[l], rwkv_w0[l], rwkv_a0[l], rwkv_k_k[l],
            rwkv_k_a[l], rk, w2p, a2p, bd, tm=n)
        s_new, y_t = _rwkv_step(state_t, l, dec, kk, bb, k2, r, v)
        y_raw = y_t.T
        lfs, _, _, _ = _foxcum(projs, 1, n, bf_pad, tm=n)
        q_b = projs[:, C_BQ:C_BQ + W_B]
        k_b = projs[:, C_BK:C_BK + W_B]
        v_b = projs[:, C_BV:C_BV + W_B]
        g_b = projs[:, C_BG:C_BG + W_B]
        yb = _diff_decode(page_flat, q_b, k_b, v_b, g_b, dk_all, dv_all, l,
                          diff_subln_g[l], diff_lq1[l], diff_lk1[l], diff_lq2[l], diff_lk2[l], lam_init)
        q_c = projs[:, C_CQ:C_CQ + W_C]
        k_c = projs[:, C_CK:C_CK + W_C]
        v_c = projs[:, C_CV:C_CV + W_C]
        g_c = projs[:, C_CG:C_CG + W_C]
        lfn_rep = jnp.broadcast_to(lfs[:, :16, None], (n, 16, LANE))
        yc = _fox_decode(page_flat, q_c, k_c, v_c, g_c, lfn_rep, fk_all, fv_all, lft_all, l)
        merged = _merge(projs, y_raw, bonus, rwkv_gn_g[l], rwkv_gn_b[l], bd, yb, yc, w_br, tm=n)
        hs = _outproj(hs, merged, w_o, tm=n)
        outs_s.append((
            k_b.reshape(n, 1, H_B, 2 * D_B), v_b.reshape(n, 1, H_B, 2 * D_B),
            k_c.reshape(n, 1, H_C, D_C), v_c.reshape(n, 1, H_C, D_C),
            lfs[:, :H_C].reshape(n, 1, H_C),
            jnp.transpose(s_new, (3, 0, 1, 2)),
            us))

    y_prompt = _rmsnorm(hp, final_g, tm=512).reshape(bsz, t, D_MODEL)
    y_sample = _rmsnorm(hs, final_g, tm=n).reshape(n, 1, D_MODEL)
    st_p = [jnp.stack([o[i] for o in outs_p]) for i in range(7)]
    st_s = [jnp.stack([o[i] for o in outs_s]) for i in range(7)]
    return (y_prompt, y_sample, st_p[0], st_s[0], st_p[1], st_s[1], st_p[2], st_s[2], st_p[3], st_s[3],
            st_p[4], st_s[4], st_p[5], st_s[5], st_p[6], st_s[6])
```

```python
import functools
import math

import jax
import jax.numpy as jnp
from jax import lax
from jax.experimental import pallas as pl
from jax.experimental.pallas import tpu as pltpu

F32 = jnp.float32
BF16 = jnp.bfloat16
HI = lax.Precision.HIGHEST

D_MODEL = 2048
DEPTH = 2
PAGE = 128
H_A, N_A = 8, 64
W_A = H_A * N_A
A_SHIFT_W = 3 * W_A + 128
H_B, D_B = 6, 64
W_B = H_B * 2 * D_B
H_C, D_C = 12, 64
W_C = H_C * D_C
NORM_EPS = 1e-6
SUBLN_EPS = 1e-5
GN_EPS = 64e-5
LANE = 128

C_GATES = 0
C_AGATE = 6144
C_U = 6656
C_BQ, C_BK, C_BV, C_BG = 8320, 9088, 9856, 10624
C_CQ, C_CK, C_CV, C_CG = 11392, 12160, 12928, 13696
C_CF = 14464
NP = 14592
TN_IN = 768

NEG = -1e30


def _cparams(sem, vmem_mb=48):
    return pltpu.CompilerParams(dimension_semantics=sem, vmem_limit_bytes=vmem_mb * 1024 * 1024)


def _softplus(z):
    return jnp.maximum(z, 0.0) + jnp.log1p(jnp.exp(-jnp.abs(z)))


def _sigmoid(z):
    return jax.nn.sigmoid(z)


def _silu(z):
    return z * jax.nn.sigmoid(z)


def _dot_nt(a, b):
    return lax.dot_general(a, b, (((1,), (1,)), ((), ())), preferred_element_type=F32)


def _inproj_kernel(x_ref, g_ref, w_ref, o_ref, h_ref):
    @pl.when(pl.program_id(1) == 0)
    def _():
        x = x_ref[...]
        ms = jnp.mean(x * x, axis=-1, keepdims=True)
        h_ref[...] = (x * lax.rsqrt(ms + NORM_EPS) * g_ref[...]).astype(BF16)

    o_ref[...] = _dot_nt(h_ref[...], w_ref[...])


def _inproj(x2d, g, w_p, tm):
    m = x2d.shape[0]
    return pl.pallas_call(
        _inproj_kernel,
        grid=(m // tm, NP // TN_IN),
        in_specs=[pl.BlockSpec((tm, D_MODEL), lambda i, j: (i, 0)),
                  pl.BlockSpec((1, D_MODEL), lambda i, j: (0, 0)),
                  pl.BlockSpec((TN_IN, D_MODEL), lambda i, j: (j, 0))],
        out_specs=pl.BlockSpec((tm, TN_IN), lambda i, j: (i, j)),
        out_shape=jax.ShapeDtypeStruct((m, NP), F32),
        scratch_shapes=[pltpu.VMEM((tm, D_MODEL), BF16)],
        compiler_params=_cparams(("parallel", "arbitrary")),
        name="inproj",
    )(x2d, g.reshape(1, D_MODEL), w_p)


def _rwkv_prep_kernel(u_ref, prev_ref, mu_ref, w0_ref, a0_ref, kk_ref, ka_ref, rk_ref,
                      w2_ref, a2_ref, bd_ref,
                      r_o, w_o, k_o, v_o, kk_o, b_o, bonus_o, *, blocks_per_seq):
    u = u_ref[...]
    if blocks_per_seq is None:
        prev = prev_ref[...]
    else:
        tm = u.shape[0]
        carry_in = jnp.where(pl.program_id(0) % blocks_per_seq == 0, 0.0, prev_ref[7:8, :])
        row = lax.broadcasted_iota(jnp.int32, (tm, 1), 0)
        prev = jnp.where(row == 0, carry_in, pltpu.roll(u, 1, axis=0))
    m = u + mu_ref[...] * (prev - u)
    r = m[:, 0:W_A]
    k = m[:, W_A:2 * W_A]
    v = m[:, 2 * W_A:3 * W_A]
    lo = m[:, 3 * W_A:3 * W_A + 128]
    zw = w0_ref[...] + jnp.dot(jnp.tanh(lo).astype(BF16), w2_ref[...], preferred_element_type=F32)
    w_log = -_softplus(-zw) - 0.5
    a = _sigmoid(a0_ref[...] + jnp.dot(lo.astype(BF16), a2_ref[...], preferred_element_type=F32))
    decay = jnp.exp(-jnp.exp(w_log))
    bd = bd_ref[...]
    kk = k * kk_ref[...]
    ss = jnp.dot(kk * kk, bd, precision=HI, preferred_element_type=F32)
    kk = kk * lax.rsqrt(jnp.maximum(ss, 1e-24))
    k2 = k * (1.0 + (a - 1.0) * ka_ref[...])
    bonus = jnp.dot(r * k2 * rk_ref[...], bd, precision=HI, preferred_element_type=F32) * v
    r_o[...] = r
    w_o[...] = decay
    k_o[...] = k2
    v_o[...] = v
    kk_o[...] = kk
    b_o[...] = kk * a
    bonus_o[...] = bonus


def _rwkv_prep(proj, prev, mu, w0, a0, k_k, k_a, r_k, w2p, a2p, bd, tm, seq_len=None):
    m = proj.shape[0]
    row = lambda n: pl.BlockSpec((1, n), lambda i: (0, 0))
    full = lambda a: pl.BlockSpec(a.shape, lambda i: (0, 0))
    tok = pl.BlockSpec((tm, W_A), lambda i: (i, 0))
    if prev is None:
        blocks_per_seq = seq_len // tm
        prev = proj
        prev_spec = pl.BlockSpec((8, A_SHIFT_W),
                                 lambda i: (jnp.maximum(i * (tm // 8) - 1, 0), C_U // A_SHIFT_W))
    else:
        blocks_per_seq = None
        prev_spec = pl.BlockSpec((tm, A_SHIFT_W), lambda i: (i, 0))
    return pl.pallas_call(
        functools.partial(_rwkv_prep_kernel, blocks_per_seq=blocks_per_seq),
        grid=(m // tm,),
        in_specs=[pl.BlockSpec((tm, A_SHIFT_W), lambda i: (i, C_U // A_SHIFT_W)),
                  prev_spec,
                  row(A_SHIFT_W), row(W_A), row(W_A), row(W_A), row(W_A), row(W_A),
                  full(w2p), full(a2p), full(bd)],
        out_specs=[tok] * 7,
        out_shape=[jax.ShapeDtypeStruct((m, W_A), F32)] * 7,
        compiler_params=_cparams(("parallel",)),
        name="rwkv_prep",
    )(proj, prev, mu.reshape(1, -1), w0.reshape(1, -1), a0.reshape(1, -1), k_k.reshape(1, -1),
      k_a.reshape(1, -1), r_k.reshape(1, -1), w2p, a2p, bd)


V_HI = 16
SCAN_TC = 16


def _rwkv_scan_kernel(w_ref, kk_ref, b_ref, k_ref, r_ref, v_ref, s0_ref, y_ref, s_ref):
    @pl.when(pl.program_id(0) == 0)
    def _():
        s_ref[...] = s0_ref[...]

    def step(t, carry):
        w = w_ref[t]
        kk = kk_ref[t]
        b = b_ref[t]
        kv = k_ref[t]
        r = r_ref[t]
        vt = v_ref[t]
        rows = []
        for vh in range(V_HI):
            s = s_ref[vh]
            sk = jnp.sum(s * kk, axis=0, keepdims=True)
            s2 = s * w - sk * b + vt[vh:vh + 1, :] * kv
            s_ref[vh] = s2
            rows.append(jnp.sum(s2 * r, axis=0, keepdims=True))
        y_ref[t] = jnp.concatenate(rows, axis=0)
        return carry

    lax.fori_loop(0, SCAN_TC, step, 0)


def _rwkv_scan(w_t, kk_t, b_t, k_t, r_t, v_t, s0):
    t = w_t.shape[0]
    tile = pl.BlockSpec((SCAN_TC, N_A, LANE), lambda i: (i, 0, 0))
    vspec = pl.BlockSpec((SCAN_TC, V_HI, LANE), lambda i: (i, 0, 0))
    sspec = pl.BlockSpec((V_HI, N_A, LANE), lambda i: (0, 0, 0))
    return pl.pallas_call(
        _rwkv_scan_kernel,
        grid=(t // SCAN_TC,),
        in_specs=[tile] * 5 + [vspec, sspec],
        out_specs=[vspec, sspec],
        out_shape=[jax.ShapeDtypeStruct((t, V_HI, LANE), F32),
                   jax.ShapeDtypeStruct((V_HI, N_A, LANE), F32)],
        compiler_params=_cparams(("arbitrary",)),
        name="rwkv_scan",
    )(w_t, kk_t, b_t, k_t, r_t, v_t, s0)


RT_C = 128


def _retile_kernel(*refs, bsz, is_v):
    ins = refs[:bsz]
    o_ref, z_ref = refs[bsz:]
    for b in range(bsz):
        xt = ins[b][...].T
        for h in range(H_A):
            z_ref[b * H_A + h] = xt[h * N_A:(h + 1) * N_A, :]
    if is_v:
        for vh in range(V_HI):
            rows = [z_ref[:, vh * (N_A // V_HI) + vl, :] for vl in range(N_A // V_HI)]
            o_ref[:, vh, :] = jnp.concatenate(rows, axis=0).T
    else:
        reps = LANE // (bsz * H_A)
        for c in range(N_A):
            o_ref[:, c, :] = jnp.concatenate([z_ref[:, c, :]] * reps, axis=0).T


def _retile(x, bsz, t, is_v):
    nt = t // RT_C
    rows = V_HI if is_v else N_A
    return pl.pallas_call(
        functools.partial(_retile_kernel, bsz=bsz, is_v=is_v),
        grid=(nt,),
        in_specs=[pl.BlockSpec((RT_C, W_A), functools.partial(lambda i, b: (b * nt + i, 0), b=b))
                  for b in range(bsz)],
        out_specs=pl.BlockSpec((RT_C, rows, LANE), lambda i: (i, 0, 0)),
        out_shape=jax.ShapeDtypeStruct((t, rows, LANE), F32),
        scratch_shapes=[pltpu.VMEM((bsz * H_A, N_A, RT_C), F32)],
        compiler_params=_cparams(("parallel",)),
        name="rwkv_retile_v" if is_v else "rwkv_retile",
    )(*([x] * bsz))


def _to_scan_tiles(x, bsz, t):
    return _retile(x, bsz, t, False)


def _to_scan_v(v, bsz, t):
    return _retile(v, bsz, t, True)


def _from_scan_y(y, bsz, t):
    y = y.reshape(t, V_HI, N_A // V_HI, bsz, H_A).transpose(3, 0, 4, 1, 2)
    return y.reshape(bsz * t, W_A)


def _from_scan_state(s, bsz):
    s = s.reshape(V_HI, N_A, N_A // V_HI, bsz, H_A).transpose(3, 4, 0, 2, 1)
    return s.reshape(bsz, H_A, N_A, N_A)


def _rwkv_step_kernel(s_ref, w_ref, kk_ref, b_ref, k_ref, r_ref, v_ref, so_ref, y_ref):
    w = w_ref[...]
    kk = kk_ref[...]
    b = b_ref[...]
    kv = k_ref[...]
    r = r_ref[...]

    def body(v, carry):
        s = s_ref[v]
        sk = jnp.sum(s * kk, axis=0, keepdims=True)
        s2 = s * w - sk * b + v_ref[pl.ds(v, 1), :] * kv
        so_ref[v] = s2
        y_ref[pl.ds(v, 1), :] = jnp.sum(s2 * r, axis=0, keepdims=True)
        return carry

    lax.fori_loop(0, N_A, body, 0, unroll=4)


def _rwkv_step(state_t, layer, w, kk, b, k2, r, v):
    n = w.shape[0]
    tile = pl.BlockSpec((N_A, n), lambda h: (h, 0))
    return pl.pallas_call(
        _rwkv_step_kernel,
        grid=(H_A,),
        in_specs=[pl.BlockSpec((None, None, N_A, N_A, n), lambda h: (layer, h, 0, 0, 0)),
                  tile, tile, tile, tile, tile, tile],
        out_specs=[pl.BlockSpec((None, N_A, N_A, n), lambda h: (h, 0, 0, 0)), tile],
        out_shape=[jax.ShapeDtypeStruct((H_A, N_A, N_A, n), F32),
                   jax.ShapeDtypeStruct((W_A, n), F32)],
        compiler_params=_cparams(("parallel",)),
        name="rwkv_step",
    )(state_t, w.T, kk.T, b.T, k2.T, r.T, v.T)


def _lam(lq1, lk1, lq2, lk2, lam_init):
    return (jnp.exp(jnp.sum(lq1[...] * lk1[...], axis=-1, keepdims=True))
            - jnp.exp(jnp.sum(lq2[...] * lk2[...], axis=-1, keepdims=True)) + lam_init)


def _online(s, m, l):
    m_new = jnp.maximum(m, jnp.max(s, axis=-1, keepdims=True))
    alpha = jnp.exp(m - m_new)
    p = jnp.exp(s - m_new)
    return p, alpha, m_new, l * alpha + jnp.sum(p, axis=-1, keepdims=True)


def _diff_attn_kernel(q_ref, k_ref, v_ref, g_ref, sg_ref, lq1, lk1, lq2, lk2, o_ref, ko_ref, vo_ref,
                      *, lam_init, tq, tk):
    qi = pl.program_id(2)

    @pl.when(qi == 0)
    def _():
        ko_ref[...] = k_ref[...]
        vo_ref[...] = v_ref[...]

    lane = lax.broadcasted_iota(jnp.int32, (1, LANE), 1)
    q = q_ref[...] * (D_B ** -0.5)
    q1 = jnp.where(lane < D_B, q, 0.0).astype(BF16)
    q2 = jnp.where(lane >= D_B, q, 0.0).astype(BF16)
    rows = qi * tq + lax.broadcasted_iota(jnp.int32, (tq, tk), 0)
    cols0 = lax.broadcasted_iota(jnp.int32, (tq, tk), 1)

    def body(j, carry):
        m1, l1, a1, m2, l2, a2 = carry
        off = pl.multiple_of(j * tk, tk)
        kb = k_ref[pl.ds(off, tk), :].astype(BF16)
        vb = v_ref[pl.ds(off, tk), :].astype(BF16)
        mask = (cols0 + off) <= rows
        s1 = jnp.where(mask, _dot_nt(q1, kb), NEG)
        s2 = jnp.where(mask, _dot_nt(q2, kb), NEG)
        p1, al1, m1, l1 = _online(s1, m1, l1)
        p2, al2, m2, l2 = _online(s2, m2, l2)
        a1 = a1 * al1 + jnp.dot(p1.astype(BF16), vb, preferred_element_type=F32)
        a2 = a2 * al2 + jnp.dot(p2.astype(BF16), vb, preferred_element_type=F32)
        return m1, l1, a1, m2, l2, a2

    mi = jnp.full((tq, 1), NEG, F32)
    zi = jnp.zeros((tq, 1), F32)
    ai = jnp.zeros((tq, LANE), F32)
    nkv = (qi * tq + tq + tk - 1) // tk
    m1, l1, a1, m2, l2, a2 = lax.fori_loop(0, nkv, body, (mi, zi, ai, mi, zi, ai))
    lam = _lam(lq1, lk1, lq2, lk2, lam_init)
    o = a1 / l1 - lam * (a2 / l2)
    o = o * lax.rsqrt(jnp.mean(o * o, axis=-1, keepdims=True) + SUBLN_EPS) * sg_ref[...]
    o_ref[...] = o * (1.0 - lam_init) * _silu(g_ref[...])


def _diff_attn(proj, bsz, t, subln_g, lq1, lk1, lq2, lk2, lam_init, tq=512, tk=512):
    nq = t // tq
    small = pl.BlockSpec((1, D_B), lambda b, h, i: (0, 0))
    return pl.pallas_call(
        functools.partial(_diff_attn_kernel, lam_init=lam_init, tq=tq, tk=tk),
        grid=(bsz, H_B, nq),
        in_specs=[pl.BlockSpec((tq, LANE), lambda b, h, i: (b * nq + i, C_BQ // LANE + h)),
                  pl.BlockSpec((t, LANE), lambda b, h, i: (b, C_BK // LANE + h)),
                  pl.BlockSpec((t, LANE), lambda b, h, i: (b, C_BV // LANE + h)),
                  pl.BlockSpec((tq, LANE), lambda b, h, i: (b * nq + i, C_BG // LANE + h)),
                  pl.BlockSpec((1, LANE), lambda b, h, i: (0, 0)),
                  small, small, small, small],
        out_specs=[pl.BlockSpec((tq, LANE), lambda b, h, i: (b * nq + i, h)),
                   pl.BlockSpec((None, None, t, LANE), lambda b, h, i: (b, h, 0, 0)),
                   pl.BlockSpec((None, None, t, LANE), lambda b, h, i: (b, h, 0, 0))],
        out_shape=[jax.ShapeDtypeStruct((bsz * t, W_B), F32),
                   jax.ShapeDtypeStruct((bsz, H_B, t, LANE), F32),
                   jax.ShapeDtypeStruct((bsz, H_B, t, LANE), F32)],
        compiler_params=_cparams(("parallel", "parallel", "arbitrary")),
        name="diff_attn",
    )(proj, proj, proj, proj, subln_g.reshape(1, -1), lq1.reshape(1, -1), lk1.reshape(1, -1),
      lq2.reshape(1, -1), lk2.reshape(1, -1))


def _foxcum_kernel(cf_ref, bf_ref, tri_ref, lf_ref, cum_ref, cumt_ref, lft_ref, carry_ref, *, tm):
    @pl.when(pl.program_id(1) == 0)
    def _():
        carry_ref[...] = jnp.zeros_like(carry_ref)

    lf = -_softplus(-(cf_ref[...] + bf_ref[...]))
    lf_ref[...] = lf
    lft_ref[...] = lf.T[:16, :]
    c = jnp.dot(tri_ref[...], lf, precision=HI, preferred_element_type=F32) + carry_ref[...]
    cum_ref[...] = c
    carry_ref[...] = c[tm - 1:tm, :]
    cumt_ref[...] = c.T[:16, :]


def _foxcum(proj, bsz, t, bf_pad, tm):
    nt = t // tm
    tri = (jnp.arange(tm)[:, None] >= jnp.arange(tm)[None, :]).astype(F32)
    tok = pl.BlockSpec((tm, LANE), lambda b, i: (b * nt + i, 0))
    return pl.pallas_call(
        functools.partial(_foxcum_kernel, tm=tm),
        grid=(bsz, nt),
        in_specs=[pl.BlockSpec((tm, LANE), lambda b, i: (b * nt + i, C_CF // LANE)),
                  pl.BlockSpec((1, LANE), lambda b, i: (0, 0)),
                  pl.BlockSpec((tm, tm), lambda b, i: (0, 0))],
        out_specs=[tok, tok, pl.BlockSpec((None, 16, tm), lambda b, i: (b, 0, i)),
                   pl.BlockSpec((None, 16, tm), lambda b, i: (b, 0, i))],
        out_shape=[jax.ShapeDtypeStruct((bsz * t, LANE), F32),
                   jax.ShapeDtypeStruct((bsz * t, LANE), F32),
                   jax.ShapeDtypeStruct((bsz, 16, t), F32),
                   jax.ShapeDtypeStruct((bsz, 16, t), F32)],
        scratch_shapes=[pltpu.VMEM((1, LANE), F32)],
        compiler_params=_cparams(("parallel", "arbitrary")),
        name="fox_cum",
    )(proj, bf_pad, tri)


def _fox_attn_kernel(q_ref, k_ref, v_ref, g_ref, cum_ref, cumt_ref, o_ref, kto_ref, vto_ref, *, tq, tk):
    hp = pl.program_id(1)
    qi = pl.program_id(2)

    @pl.when(qi == 0)
    def _():
        kto_ref[...] = k_ref[...].T
        vto_ref[...] = v_ref[...].T

    lane = lax.broadcasted_iota(jnp.int32, (1, LANE), 1)
    q = q_ref[...] * (D_C ** -0.5)
    qa = jnp.where(lane < D_C, q, 0.0).astype(BF16)
    qb = jnp.where(lane >= D_C, q, 0.0).astype(BF16)
    cq = cum_ref[...]
    cqa = jnp.sum(jnp.where(lane == 2 * hp, cq, 0.0), axis=-1, keepdims=True)
    cqb = jnp.sum(jnp.where(lane == 2 * hp + 1, cq, 0.0), axis=-1, keepdims=True)
    rows = qi * tq + lax.broadcasted_iota(jnp.int32, (tq, tk), 0)
    cols0 = lax.broadcasted_iota(jnp.int32, (tq, tk), 1)

    def body(j, carry):
        ma, la, aa, mb, lb, ab = carry
        off = pl.multiple_of(j * tk, tk)
        kb = k_ref[pl.ds(off, tk), :].astype(BF16)
        vb = v_ref[pl.ds(off, tk), :].astype(BF16)
        cka = cumt_ref[pl.ds(2 * hp, 1), pl.ds(off, tk)]
        ckb = cumt_ref[pl.ds(2 * hp + 1, 1), pl.ds(off, tk)]
        mask = (cols0 + off) <= rows
        sa = jnp.where(mask, _dot_nt(qa, kb) + (cqa - cka), NEG)
        sb = jnp.where(mask, _dot_nt(qb, kb) + (cqb - ckb), NEG)
        pa, ala, ma, la = _online(sa, ma, la)
        pb, alb, mb, lb = _online(sb, mb, lb)
        aa = aa * ala + jnp.dot(pa.astype(BF16), vb, preferred_element_type=F32)
        ab = ab * alb + jnp.dot(pb.astype(BF16), vb, preferred_element_type=F32)
        return ma, la, aa, mb, lb, ab

    mi = jnp.full((tq, 1), NEG, F32)
    zi = jnp.zeros((tq, 1), F32)
    ai = jnp.zeros((tq, LANE), F32)
    nkv = (qi * tq + tq + tk - 1) // tk
    ma, la, aa, mb, lb, ab = lax.fori_loop(0, nkv, body, (mi, zi, ai, mi, zi, ai))
    o = jnp.where(lane < D_C, aa / la, ab / lb)
    o_ref[...] = o * _silu(g_ref[...])


def _fox_attn(proj, cum, cumt, bsz, t, tq=512, tk=512):
    nq = t // tq
    return pl.pallas_call(
        functools.partial(_fox_attn_kernel, tq=tq, tk=tk),
        grid=(bsz, H_C // 2, nq),
        in_specs=[pl.BlockSpec((tq, LANE), lambda b, h, i: (b * nq + i, C_CQ // LANE + h)),
                  pl.BlockSpec((t, LANE), lambda b, h, i: (b, C_CK // LANE + h)),
                  pl.BlockSpec((t, LANE), lambda b, h, i: (b, C_CV // LANE + h)),
                  pl.BlockSpec((tq, LANE), lambda b, h, i: (b * nq + i, C_CG // LANE + h)),
                  pl.BlockSpec((tq, LANE), lambda b, h, i: (b * nq + i, 0)),
                  pl.BlockSpec((None, 16, t), lambda b, h, i: (b, 0, 0))],
        out_specs=[pl.BlockSpec((tq, LANE), lambda b, h, i: (b * nq + i, h)),
                   pl.BlockSpec((None, None, LANE, t), lambda b, h, i: (b, h, 0, 0)),
                   pl.BlockSpec((None, None, LANE, t), lambda b, h, i: (b, h, 0, 0))],
        out_shape=[jax.ShapeDtypeStruct((bsz * t, W_C), F32),
                   jax.ShapeDtypeStruct((bsz, H_C // 2, LANE, t), F32),
                   jax.ShapeDtypeStruct((bsz, H_C // 2, LANE, t), F32)],
        compiler_params=_cparams(("parallel", "parallel", "arbitrary")),
        name="fox_attn",
    )(proj, proj, proj, proj, cum, cumt)


PP = 8
N_PAGES = 16
NSTEP = N_PAGES // PP
DEC_GROUP = 4


def _diff_dec_kernel(pt_ref, q_ref, kn_ref, vn_ref, g_ref, sg_ref, lq1, lk1, lq2, lk2, *rest,
                     lam_init):
    kp = rest[:PP]
    vp = rest[PP:2 * PP]
    o_ref, qrep_ref, m_ref, l_ref, acc_ref = rest[2 * PP:]
    s = pl.program_id(1)
    lo = lax.broadcasted_iota(jnp.int32, (1, LANE), 1) < D_B

    def spread(x):
        sw = pltpu.roll(x, D_B, axis=1)
        return jnp.where(lo, x, sw), jnp.where(lo, sw, x)

    @pl.when(s == 0)
    def _():
        qrow = q_ref[...] * (D_B ** -0.5)
        r = lax.broadcasted_iota(jnp.int32, (LANE, LANE), 0)
        c = lax.broadcasted_iota(jnp.int32, (LANE, LANE), 1)
        for h in range(H_B):
            qh = qrow[:, h * LANE:(h + 1) * LANE]
            col = jnp.sum(jnp.where(r == c, qh, 0.0), axis=1, keepdims=True)
            same_map = (r < D_B) == (c < D_B)
            qrep_ref[h] = jnp.where(same_map, jnp.broadcast_to(col, (LANE, LANE)), 0.0).astype(BF16)
        m_ref[...] = jnp.full_like(m_ref, NEG)
        l_ref[...] = jnp.zeros_like(l_ref)
        acc_ref[...] = jnp.zeros_like(acc_ref)

    for h in range(H_B):
        w = qrep_ref[h]
        m_run = m_ref[h][0:1, :]
        acc1 = acc_ref[2 * h]
        acc2 = acc_ref[2 * h + 1]
        l = l_ref[h]
        for g0 in range(0, PP, DEC_GROUP):
            pages = range(g0, g0 + DEC_GROUP)
            ss = [jnp.dot(kp[j][h].astype(BF16), w, preferred_element_type=F32) for j in pages]
            mx = ss[0]
            for x in ss[1:]:
                mx = jnp.maximum(mx, x)
            m_new = jnp.maximum(m_run, jnp.max(mx, axis=0, keepdims=True))
            alpha = jnp.exp(m_run - m_new)
            al1, al2 = spread(alpha)
            acc1 = acc1 * al1
            acc2 = acc2 * al2
            l = l * alpha
            for x, j in zip(ss, pages):
                p = jnp.exp(x - m_new)
                p1, p2 = spread(p)
                v = vp[j][h]
                acc1 = acc1 + p1 * v
                acc2 = acc2 + p2 * v
                l = l + p
            m_run = m_new
        acc_ref[2 * h] = acc1
        acc_ref[2 * h + 1] = acc2
        l_ref[h] = l
        m_ref[h] = jnp.broadcast_to(m_run, (8, LANE))

    @pl.when(s == NSTEP - 1)
    def _():
        lam = _lam(lq1, lk1, lq2, lk2, lam_init)
        for h in range(H_B):
            knh = jnp.broadcast_to(kn_ref[:, h * LANE:(h + 1) * LANE], (8, LANE)).astype(BF16)
            s_new = jnp.dot(knh, qrep_ref[h], preferred_element_type=F32)[0:1, :]
            vnh = vn_ref[:, h * LANE:(h + 1) * LANE]
            m_old = m_ref[h][0:1, :]
            m_f = jnp.maximum(m_old, s_new)
            alpha = jnp.exp(m_old - m_f)
            pn = jnp.exp(s_new - m_f)
            den = jnp.sum(l_ref[h], axis=0, keepdims=True) * alpha + pn
            als, pns, dens = spread(alpha), spread(pn), spread(den)
            outs = [(jnp.sum(acc_ref[2 * h + mp], axis=0, keepdims=True) * als[mp] + pns[mp] * vnh)
                    / dens[mp] for mp in range(2)]
            od = outs[0] - lam * outs[1]
            od = od * lax.rsqrt(jnp.mean(od * od, axis=-1, keepdims=True) + SUBLN_EPS) * sg_ref[...]
            gate = g_ref[:, h * LANE:(h + 1) * LANE]
            o_ref[:, h * LANE:(h + 1) * LANE] = od * (1.0 - lam_init) * _silu(gate)


def _diff_decode(page_flat, q, kn, vn, g, cache_k, cache_v, layer, subln_g, lq1, lk1, lq2, lk2,
                 lam_init):
    n = q.shape[0]
    full = pl.BlockSpec((None, 1, W_B), lambda b, s, pt: (b, 0, 0))
    small = pl.BlockSpec((1, D_B), lambda b, s, pt: (0, 0))
    seq = lambda x: x.reshape(n, 1, W_B)

    def page_spec(j):
        return pl.BlockSpec((None, None, H_B, PAGE, 2 * D_B),
                            lambda b, s, pt: (layer, pt[b * N_PAGES + s * PP + j], 0, 0, 0))

    grid_spec = pltpu.PrefetchScalarGridSpec(
        num_scalar_prefetch=1,
        grid=(n, NSTEP),
        in_specs=[full, full, full, full, pl.BlockSpec((1, LANE), lambda b, s, pt: (0, 0)),
                  small, small, small, small]
                 + [page_spec(j) for j in range(PP)] + [page_spec(j) for j in range(PP)],
        out_specs=full,
        scratch_shapes=[pltpu.VMEM((H_B, LANE, LANE), BF16),
                        pltpu.VMEM((H_B, 8, LANE), F32),
                        pltpu.VMEM((H_B, PAGE, LANE), F32),
                        pltpu.VMEM((2 * H_B, PAGE, LANE), F32)],
    )
    return pl.pallas_call(
        functools.partial(_diff_dec_kernel, lam_init=lam_init),
        grid_spec=grid_spec,
        out_shape=jax.ShapeDtypeStruct((n, 1, W_B), F32),
        compiler_params=_cparams(("arbitrary", "arbitrary")),
        name="diff_decode",
    )(page_flat, seq(q), seq(kn), seq(vn), seq(g), subln_g.reshape(1, -1), lq1.reshape(1, -1),
      lk1.reshape(1, -1), lq2.reshape(1, -1), lk2.reshape(1, -1),
      *([cache_k] * PP), *([cache_v] * PP)).reshape(n, W_B)


def _fox_dec_kernel(pt_ref, q_ref, kn_ref, vn_ref, g_ref, lfn_ref, tri_ref, *rest):
    kp = rest[:PP]
    vp = rest[PP:2 * PP]
    lfp = rest[2 * PP:3 * PP]
    o_ref, qcol_ref, m_ref, l_ref, c_ref, acc_ref = rest[3 * PP:]
    b = pl.program_id(0)
    s = pl.program_id(1)

    @pl.when(s == 0)
    def _():
        qrow = q_ref[...] * (D_C ** -0.5)
        r = lax.broadcasted_iota(jnp.int32, (D_C, D_C), 0)
        c = lax.broadcasted_iota(jnp.int32, (D_C, D_C), 1)
        for h in range(H_C):
            qh = qrow[:, h * D_C:(h + 1) * D_C]
            col = jnp.sum(jnp.where(r == c, qh, 0.0), axis=1, keepdims=True)
            qcol_ref[h] = jnp.broadcast_to(col, (D_C, PAGE))
        m_ref[...] = jnp.full_like(m_ref, NEG)
        l_ref[...] = jnp.zeros_like(l_ref)
        c_ref[...] = jnp.zeros_like(c_ref)
        acc_ref[...] = jnp.zeros_like(acc_ref)

    tri = tri_ref[...]
    carry = c_ref[...][:, :1]
    row16 = lax.broadcasted_iota(jnp.int32, (16, PAGE), 0)
    tiles = []
    for j in range(PP):
        sub = pt_ref[b * N_PAGES + s * PP + j] % 8
        sc_t = jnp.zeros((16, PAGE), F32)
        lf = jnp.zeros((16, PAGE), F32)
        for h in range(H_C):
            sh = jnp.sum(kp[j][h] * qcol_ref[h], axis=0, keepdims=True)
            sc_t = jnp.where(row16 == h, sh, sc_t)
            lf = jnp.where(row16 == h, lfp[j][h, pl.ds(sub, 1), :], lf)
        cum = jnp.dot(lf, tri, precision=HI, preferred_element_type=F32) + carry
        carry = carry + jnp.sum(lf, axis=-1, keepdims=True)
        tiles.append(sc_t - cum)
    sc = jnp.concatenate(tiles, axis=1)
    p, alpha, m_new, l_new = _online(sc, m_ref[...][:, :1], l_ref[...][:, :1])
    m_ref[...] = jnp.broadcast_to(m_new, (16, LANE))
    l_ref[...] = jnp.broadcast_to(l_new, (16, LANE))
    c_ref[...] = jnp.broadcast_to(carry, (16, LANE))
    for h in range(H_C):
        acc = acc_ref[h] * alpha[h:h + 1, :]
        for j in range(PP):
            acc = acc + vp[j][h] * p[h:h + 1, j * PAGE:(j + 1) * PAGE]
        acc_ref[h] = acc

    @pl.when(s == NSTEP - 1)
    def _():
        qrow = q_ref[...] * (D_C ** -0.5)
        rcol = lax.broadcasted_iota(jnp.int32, (16, 1), 0)
        s_new = jnp.zeros((16, 1), F32)
        for h in range(H_C):
            dot = jnp.sum(qrow[:, h * D_C:(h + 1) * D_C] * kn_ref[:, h * D_C:(h + 1) * D_C],
                          axis=-1, keepdims=True)
            s_new = jnp.where(rcol == h, dot, s_new)
        s_new = s_new - (carry + lfn_ref[...][:, :1])
        m_f = jnp.maximum(m_new, s_new)
        al = jnp.exp(m_new - m_f)
        pn = jnp.exp(s_new - m_f)
        l_f = l_new * al + pn
        ones = jnp.ones((8, PAGE), F32)
        tot = lax.dot_general(ones, acc_ref[...].reshape(H_C * D_C, PAGE), (((1,), (1,)), ((), ())),
                              precision=HI, preferred_element_type=F32)[0:1, :]
        outs = []
        for h in range(H_C):
            past = tot[:, h * D_C:(h + 1) * D_C]
            vnh = vn_ref[:, h * D_C:(h + 1) * D_C]
            outs.append((past * al[h:h + 1, :] + pn[h:h + 1, :] * vnh) / l_f[h:h + 1, :])
        o_all = jnp.concatenate(outs, axis=1)
        o_ref[...] = o_all * _silu(g_ref[...])


def _fox_decode(page_flat, q, kn, vn, g, lfn_rep, cache_k, cache_v, cache_lft, layer):
    n = q.shape[0]
    full = pl.BlockSpec((None, 1, W_C), lambda b, s, pt: (b, 0, 0))
    seq = lambda x: x.reshape(n, 1, W_C)
    tri = (jnp.arange(PAGE)[:, None] <= jnp.arange(PAGE)[None, :]).astype(F32)

    def page_spec(j):
        return pl.BlockSpec((None, None, H_C, D_C, PAGE),
                            lambda b, s, pt: (layer, pt[b * N_PAGES + s * PP + j], 0, 0, 0))

    def lf_spec(j):
        return pl.BlockSpec((None, H_C, None, 8, PAGE),
                            lambda b, s, pt: (layer, 0, pt[b * N_PAGES + s * PP + j] // 8, 0, 0))

    grid_spec = pltpu.PrefetchScalarGridSpec(
        num_scalar_prefetch=1,
        grid=(n, NSTEP),
        in_specs=[full, full, full, full,
                  pl.BlockSpec((None, 16, LANE), lambda b, s, pt: (b, 0, 0)),
                  pl.BlockSpec((PAGE, PAGE), lambda b, s, pt: (0, 0))]
                 + [page_spec(j) for j in range(PP)] + [page_spec(j) for j in range(PP)]
                 + [lf_spec(j) for j in range(PP)],
        out_specs=full,
        scratch_shapes=[pltpu.VMEM((H_C, D_C, PAGE), F32)] + [pltpu.VMEM((16, LANE), F32)] * 3
                       + [pltpu.VMEM((H_C, D_C, PAGE), F32)],
    )
    return pl.pallas_call(
        _fox_dec_kernel,
        grid_spec=grid_spec,
        out_shape=jax.ShapeDtypeStruct((n, 1, W_C), F32),
        compiler_params=_cparams(("arbitrary", "arbitrary")),
        name="fox_decode",
    )(page_flat, seq(q), seq(kn), seq(vn), seq(g), lfn_rep, tri, *([cache_k] * PP),
      *([cache_v] * PP), *([cache_lft] * PP)).reshape(n, W_C)


def _merge_kernel(ga_ref, gb_ref, gc_ref, ag_ref, y_ref, bonus_ref, gng_ref, gnb_ref, bd_ref,
                  yb_ref, yc_ref, w_ref, o_ref):
    y = y_ref[...]
    bd = bd_ref[...]
    mean = jnp.dot(y, bd, precision=HI, preferred_element_type=F32) * (1.0 / N_A)
    d = y - mean
    var = jnp.dot(d * d, bd, precision=HI, preferred_element_type=F32) * (1.0 / N_A)
    yn = d * lax.rsqrt(var + GN_EPS) * gng_ref[...] + gnb_ref[...] + bonus_ref[...]
    ya = yn * _silu(ag_ref[...])
    pa = jnp.dot(ya.astype(BF16), w_ref[0:W_A, :], preferred_element_type=F32)
    pb = jnp.dot(yb_ref[...].astype(BF16), w_ref[W_A:W_A + W_B, :], preferred_element_type=F32)
    pc = jnp.dot(yc_ref[...].astype(BF16), w_ref[W_A + W_B:, :], preferred_element_type=F32)
    merged = _sigmoid(ga_ref[...]) * pa + _sigmoid(gb_ref[...]) * pb + _sigmoid(gc_ref[...]) * pc
    o_ref[...] = merged.astype(BF16)


def _merge(proj, y_raw, bonus, gn_g, gn_b, bd, yb, yc, w_br, tm):
    m = proj.shape[0]
    gate = lambda c: pl.BlockSpec((tm, D_MODEL), lambda i: (i, c))
    tok = lambda n: pl.BlockSpec((tm, n), lambda i: (i, 0))
    row = pl.BlockSpec((1, W_A), lambda i: (0, 0))
    return pl.pallas_call(
        _merge_kernel,
        grid=(m // tm,),
        in_specs=[gate(0), gate(1), gate(2),
                  pl.BlockSpec((tm, W_A), lambda i: (i, C_AGATE // W_A)),
                  tok(W_A), tok(W_A), row, row, pl.BlockSpec((W_A, W_A), lambda i: (0, 0)),
                  tok(W_B), tok(W_C), pl.BlockSpec((D_MODEL, D_MODEL), lambda i: (0, 0))],
        out_specs=tok(D_MODEL),
        out_shape=jax.ShapeDtypeStruct((m, D_MODEL), BF16),
        compiler_params=_cparams(("parallel",)),
        name="merge",
    )(proj, proj, proj, proj, y_raw, bonus, gn_g.reshape(1, -1), gn_b.reshape(1, -1), bd, yb, yc, w_br)


def _outproj_kernel(x_ref, m_ref, w_ref, o_ref):
    o_ref[...] = x_ref[...] + jnp.dot(m_ref[...], w_ref[...], preferred_element_type=F32)


def _outproj(x2d, merged, w_out, tm):
    m = x2d.shape[0]
    tok = pl.BlockSpec((tm, D_MODEL), lambda i: (i, 0))
    return pl.pallas_call(
        _outproj_kernel,
        grid=(m // tm,),
        in_specs=[tok, tok, pl.BlockSpec((D_MODEL, D_MODEL), lambda i: (0, 0))],
        out_specs=tok,
        out_shape=jax.ShapeDtypeStruct((m, D_MODEL), F32),
        compiler_params=_cparams(("parallel",)),
        name="outproj",
    )(x2d, merged, w_out)


def _rmsnorm_kernel(x_ref, g_ref, o_ref):
    x = x_ref[...]
    o_ref[...] = x * lax.rsqrt(jnp.mean(x * x, axis=-1, keepdims=True) + NORM_EPS) * g_ref[...]


def _rmsnorm(x2d, g, tm):
    m = x2d.shape[0]
    tok = pl.BlockSpec((tm, D_MODEL), lambda i: (i, 0))
    return pl.pallas_call(
        _rmsnorm_kernel,
        grid=(m // tm,),
        in_specs=[tok, pl.BlockSpec((1, D_MODEL), lambda i: (0, 0))],
        out_specs=tok,
        out_shape=jax.ShapeDtypeStruct((m, D_MODEL), F32),
        compiler_params=_cparams(("parallel",)),
        name="final_norm",
    )(x2d, g.reshape(1, -1))


def _w_row_offsets():
    segs = [(8332, 6144), (1664, 512), (0, 1664), (2176, 5376), (7564, 768), (7552, LANE)]
    return [start + k for start, size in segs for k in range(0, size, LANE)]


def _wperm_kernel(offs_ref, w_ref, o0_ref, o1_ref):
    o0_ref[...] = w_ref[:, 0, :].astype(BF16)
    o1_ref[...] = w_ref[:, 1, :].astype(BF16)


def _permute_w_in(w_in_t):
    assert w_in_t.shape[1] == 2
    offs = jnp.asarray(_w_row_offsets(), jnp.int32)
    grid_spec = pltpu.PrefetchScalarGridSpec(
        num_scalar_prefetch=1,
        grid=(NP // LANE,),
        in_specs=[pl.BlockSpec((pl.Element(LANE), pl.Element(2), pl.Element(D_MODEL)),
                               lambda d, offs: (offs[d], 0, 0))],
        out_specs=[pl.BlockSpec((LANE, D_MODEL), lambda d, offs: (d, 0))] * 2,
    )
    return pl.pallas_call(
        _wperm_kernel,
        grid_spec=grid_spec,
        out_shape=[jax.ShapeDtypeStruct((NP, D_MODEL), BF16)] * 2,
        compiler_params=_cparams(("arbitrary",)),
        name="w_in_permute",
    )(offs, w_in_t)


def kernel(x_prompt, x_sample, cache_diff_k, cache_diff_v, cache_fox_k, cache_fox_v, cache_fox_logf,
           state_rwkv_wkv, state_rwkv_shift, page_table, norm_g, w_in, rwkv_mu, rwkv_w0, rwkv_w2,
           rwkv_a0, rwkv_a2, rwkv_k_k, rwkv_k_a, rwkv_r_k, rwkv_gn_g, rwkv_gn_b, diff_lq1, diff_lk1,
           diff_lq2, diff_lk2, diff_subln_g, fox_b_f, w_branch, w_out, final_g):
    bsz, t, _ = x_prompt.shape
    n = x_sample.shape[0]
    mp = bsz * t
    hp = x_prompt.reshape(mp, D_MODEL)
    hs = x_sample.reshape(n, D_MODEL)
    page_flat = page_table.reshape(-1).astype(jnp.int32)
    bd = jnp.kron(jnp.eye(H_A, dtype=F32), jnp.ones((N_A, N_A), F32))
    n_pool = cache_fox_logf.shape[1]
    assert n_pool % 8 == 0 and n == LANE
    dk_all = jnp.transpose(cache_diff_k, (0, 1, 3, 2, 4))
    dv_all = jnp.transpose(cache_diff_v, (0, 1, 3, 2, 4))
    fk_all = jnp.transpose(cache_fox_k, (0, 1, 3, 4, 2))
    fv_all = jnp.transpose(cache_fox_v, (0, 1, 3, 4, 2))
    lft_all = jnp.transpose(cache_fox_logf, (0, 3, 1, 2)).reshape(DEPTH, H_C, n_pool // 8, 8, PAGE)
    state_t = jnp.transpose(state_rwkv_wkv, (0, 2, 3, 4, 1))
    zero_lora = jnp.zeros((64, W_A), BF16)
    w_p_all = _permute_w_in(jnp.transpose(w_in, (2, 0, 1)))
    outs_p, outs_s = [], []
    for l in range(DEPTH):
        lam_init = 0.8 - 0.6 * math.exp(-0.3 * l)
        w_p = w_p_all[l]
        w2p = jnp.concatenate([rwkv_w2[l].astype(BF16), zero_lora], axis=0)
        a2p = jnp.concatenate([zero_lora, rwkv_a2[l].astype(BF16)], axis=0)
        bf_pad = jnp.pad(fox_b_f[l], (0, LANE - H_C)).reshape(1, LANE)
        w_br = w_branch[l].astype(BF16)
        w_o = w_out[l].astype(BF16)
        rk = rwkv_r_k[l].reshape(-1)

        proj = _inproj(hp, norm_g[l], w_p, tm=1024)
        r, dec, k2, v, kk, bb, bonus = _rwkv_prep(
            proj, None, rwkv_mu[l], rwkv_w0[l], rwkv_a0[l], rwkv_k_k[l],
            rwkv_k_a[l], rk, w2p, a2p, bd, tm=512, seq_len=t)
        y_t, s_fin = _rwkv_scan(
            _to_scan_tiles(dec, bsz, t), _to_scan_tiles(kk, bsz, t), _to_scan_tiles(bb, bsz, t),
            _to_scan_tiles(k2, bsz, t), _to_scan_tiles(r, bsz, t), _to_scan_v(v, bsz, t),
            jnp.zeros((V_HI, N_A, LANE), F32))
        y_raw = _from_scan_y(y_t, bsz, t)
        yb, dk_new, dv_new = _diff_attn(proj, bsz, t, diff_subln_g[l], diff_lq1[l], diff_lk1[l],
                                        diff_lq2[l], diff_lk2[l], lam_init)
        _, cum, cumt, lft = _foxcum(proj, bsz, t, bf_pad, tm=256)
        yc, fk_new, fv_new = _fox_attn(proj, cum, cumt, bsz, t)
        merged = _merge(proj, y_raw, bonus, rwkv_gn_g[l], rwkv_gn_b[l], bd, yb, yc, w_br, tm=256)
        hp = _outproj(hp, merged, w_o, tm=512)
        outs_p.append((
            jnp.transpose(dk_new, (0, 2, 1, 3)),
            jnp.transpose(dv_new, (0, 2, 1, 3)),
            jnp.transpose(fk_new.reshape(bsz, H_C, D_C, t), (0, 3, 1, 2)),
            jnp.transpose(fv_new.reshape(bsz, H_C, D_C, t), (0, 3, 1, 2)),
            jnp.transpose(lft[:, :H_C, :], (0, 2, 1)),
            _from_scan_state(s_fin, bsz),
            proj.reshape(bsz, t, NP)[:, t - 1, C_U:C_U + A_SHIFT_W]))

        projs = _inproj(hs, norm_g[l], w_p, tm=n)
        us = projs[:, C_U:C_U + A_SHIFT_W]
        r, dec, k2, v, kk, bb, bonus = _rwkv_prep(
            projs, state_rwkv_shift[l], rwkv_mu[l], rwkv_w0[l], rwkv_a0[l], rwkv_k_k[l],
            rwkv_k_a[l], rk, w2p, a2p, bd, tm=n)
        s_new, y_t = _rwkv_step(state_t, l, dec, kk, bb, k2, r, v)
        y_raw = y_t.T
        lfs, _, _, _ = _foxcum(projs, 1, n, bf_pad, tm=n)
        q_b = projs[:, C_BQ:C_BQ + W_B]
        k_b = projs[:, C_BK:C_BK + W_B]
        v_b = projs[:, C_BV:C_BV + W_B]
        g_b = projs[:, C_BG:C_BG + W_B]
        yb = _diff_decode(page_flat, q_b, k_b, v_b, g_b, dk_all, dv_all, l,
                          diff_subln_g[l], diff_lq1[l], diff_lk1[l], diff_lq2[l], diff_lk2[l], lam_init)
        q_c = projs[:, C_CQ:C_CQ + W_C]
        k_c = projs[:, C_CK:C_CK + W_C]
        v_c = projs[:, C_CV:C_CV + W_C]
        g_c = projs[:, C_CG:C_CG + W_C]
        lfn_rep = jnp.broadcast_to(lfs[:, :16, None], (n, 16, LANE))
        yc = _fox_decode(page_flat, q_c, k_c, v_c, g_c, lfn_rep, fk_all, fv_all, lft_all, l)
        merged = _merge(projs, y_raw, bonus, rwkv_gn_g[l], rwkv_gn_b[l], bd, yb, yc, w_br, tm=n)
        hs = _outproj(hs, merged, w_o, tm=n)
        outs_s.append((
            k_b.reshape(n, 1, H_B, 2 * D_B), v_b.reshape(n, 1, H_B, 2 * D_B),
            k_c.reshape(n, 1, H_C, D_C), v_c.reshape(n, 1, H_C, D_C),
            lfs[:, :H_C].reshape(n, 1, H_C),
            jnp.transpose(s_new, (3, 0, 1, 2)),
            us))

    y_prompt = _rmsnorm(hp, final_g, tm=512).reshape(bsz, t, D_MODEL)
    y_sample = _rmsnorm(hs, final_g, tm=n).reshape(n, 1, D_MODEL)
    st_p = [jnp.stack([o[i] for o in outs_p]) for i in range(7)]
    st_s = [jnp.stack([o[i] for o in outs_s]) for i in range(7)]
    return (y_prompt, y_sample, st_p[0], st_s[0], st_p[1], st_s[1], st_p[2], st_s[2], st_p[3], st_s[3],
            st_p[4], st_s[4], st_p[5], st_s[5], st_p[6], st_s[6])
```

```python
import functools
import math

import jax
import jax.numpy as jnp
from jax import lax
from jax.experimental import pallas as pl
from jax.experimental.pallas import tpu as pltpu

F32 = jnp.float32
BF16 = jnp.bfloat16
HI = lax.Precision.HIGHEST

D_MODEL = 2048
DEPTH = 2
PAGE = 128
H_A, N_A = 8, 64
W_A = H_A * N_A
A_SHIFT_W = 3 * W_A + 128
H_B, D_B = 6, 64
W_B = H_B * 2 * D_B
H_C, D_C = 12, 64
W_C = H_C * D_C
NORM_EPS = 1e-6
SUBLN_EPS = 1e-5
GN_EPS = 64e-5
LANE = 128

C_GATES = 0
C_AGATE = 6144
C_U = 6656
C_BQ, C_BK, C_BV, C_BG = 8320, 9088, 9856, 10624
C_CQ, C_CK, C_CV, C_CG = 11392, 12160, 12928, 13696
C_CF = 14464
NP = 14592
TN_IN = 768

NEG = -1e30


def _cparams(sem, vmem_mb=48):
    return pltpu.CompilerParams(dimension_semantics=sem, vmem_limit_bytes=vmem_mb * 1024 * 1024)


def _softplus(z):
    return jnp.maximum(z, 0.0) + jnp.log1p(jnp.exp(-jnp.abs(z)))


def _sigmoid(z):
    return jax.nn.sigmoid(z)


def _silu(z):
    return z * jax.nn.sigmoid(z)


def _dot_nt(a, b):
    return lax.dot_general(a, b, (((1,), (1,)), ((), ())), preferred_element_type=F32)


def _inproj_kernel(x_ref, g_ref, w_ref, o_ref, h_ref):
    @pl.when(pl.program_id(1) == 0)
    def _():
        x = x_ref[...]
        ms = jnp.mean(x * x, axis=-1, keepdims=True)
        h_ref[...] = (x * lax.rsqrt(ms + NORM_EPS) * g_ref[...]).astype(BF16)

    o_ref[...] = _dot_nt(h_ref[...], w_ref[...])


def _inproj(x2d, g, w_p, tm):
    m = x2d.shape[0]
    return pl.pallas_call(
        _inproj_kernel,
        grid=(m // tm, NP // TN_IN),
        in_specs=[pl.BlockSpec((tm, D_MODEL), lambda i, j: (i, 0)),
                  pl.BlockSpec((1, D_MODEL), lambda i, j: (0, 0)),
                  pl.BlockSpec((TN_IN, D_MODEL), lambda i, j: (j, 0))],
        out_specs=pl.BlockSpec((tm, TN_IN), lambda i, j: (i, j)),
        out_shape=jax.ShapeDtypeStruct((m, NP), F32),
        scratch_shapes=[pltpu.VMEM((tm, D_MODEL), BF16)],
        compiler_params=_cparams(("parallel", "arbitrary")),
        name="inproj",
    )(x2d, g.reshape(1, D_MODEL), w_p)


def _rwkv_prep_kernel(u_ref, prev_ref, mu_ref, w0_ref, a0_ref, kk_ref, ka_ref, rk_ref,
                      w2_ref, a2_ref, bd_ref,
                      r_o, w_o, k_o, v_o, kk_o, b_o, bonus_o, *, blocks_per_seq):
    u = u_ref[...]
    if blocks_per_seq is None:
        prev = prev_ref[...]
    else:
        tm = u.shape[0]
        carry_in = jnp.where(pl.program_id(0) % blocks_per_seq == 0, 0.0, prev_ref[7:8, :])
        row = lax.broadcasted_iota(jnp.int32, (tm, 1), 0)
        prev = jnp.where(row == 0, carry_in, pltpu.roll(u, 1, axis=0))
    m = u + mu_ref[...] * (prev - u)
    r = m[:, 0:W_A]
    k = m[:, W_A:2 * W_A]
    v = m[:, 2 * W_A:3 * W_A]
    lo = m[:, 3 * W_A:3 * W_A + 128]
    zw = w0_ref[...] + jnp.dot(jnp.tanh(lo).astype(BF16), w2_ref[...], preferred_element_type=F32)
    w_log = -_softplus(-zw) - 0.5
    a = _sigmoid(a0_ref[...] + jnp.dot(lo.astype(BF16), a2_ref[...], preferred_element_type=F32))
    decay = jnp.exp(-jnp.exp(w_log))
    bd = bd_ref[...]
    kk = k * kk_ref[...]
    ss = jnp.dot(kk * kk, bd, precision=HI, preferred_element_type=F32)
    kk = kk * lax.rsqrt(jnp.maximum(ss, 1e-24))
    k2 = k * (1.0 + (a - 1.0) * ka_ref[...])
    bonus = jnp.dot(r * k2 * rk_ref[...], bd, precision=HI, preferred_element_type=F32) * v
    r_o[...] = r
    w_o[...] = decay
    k_o[...] = k2
    v_o[...] = v
    kk_o[...] = kk
    b_o[...] = kk * a
    bonus_o[...] = bonus


def _rwkv_prep(proj, prev, mu, w0, a0, k_k, k_a, r_k, w2p, a2p, bd, tm, seq_len=None):
    m = proj.shape[0]
    row = lambda n: pl.BlockSpec((1, n), lambda i: (0, 0))
    full = lambda a: pl.BlockSpec(a.shape, lambda i: (0, 0))
    tok = pl.BlockSpec((tm, W_A), lambda i: (i, 0))
    if prev is None:
        blocks_per_seq = seq_len // tm
        prev = proj
        prev_spec = pl.BlockSpec((8, A_SHIFT_W),
                                 lambda i: (jnp.maximum(i * (tm // 8) - 1, 0), C_U // A_SHIFT_W))
    else:
        blocks_per_seq = None
        prev_spec = pl.BlockSpec((tm, A_SHIFT_W), lambda i: (i, 0))
    return pl.pallas_call(
        functools.partial(_rwkv_prep_kernel, blocks_per_seq=blocks_per_seq),
        grid=(m // tm,),
        in_specs=[pl.BlockSpec((tm, A_SHIFT_W), lambda i: (i, C_U // A_SHIFT_W)),
                  prev_spec,
                  row(A_SHIFT_W), row(W_A), row(W_A), row(W_A), row(W_A), row(W_A),
                  full(w2p), full(a2p), full(bd)],
        out_specs=[tok] * 7,
        out_shape=[jax.ShapeDtypeStruct((m, W_A), F32)] * 7,
        compiler_params=_cparams(("parallel",)),
        name="rwkv_prep",
    )(proj, prev, mu.reshape(1, -1), w0.reshape(1, -1), a0.reshape(1, -1), k_k.reshape(1, -1),
      k_a.reshape(1, -1), r_k.reshape(1, -1), w2p, a2p, bd)


V_HI = 16
SCAN_TC = 16


def _rwkv_scan_kernel(w_ref, kk_ref, b_ref, k_ref, r_ref, v_ref, s0_ref, y_ref, s_ref):
    @pl.when(pl.program_id(0) == 0)
    def _():
        s_ref[...] = s0_ref[...]

    def step(t, carry):
        w = w_ref[t]
        kk = kk_ref[t]
        b = b_ref[t]
        kv = k_ref[t]
        r = r_ref[t]
        vt = v_ref[t]
        rows = []
        for vh in range(V_HI):
            s = s_ref[vh]
            sk = jnp.sum(s * kk, axis=0, keepdims=True)
            s2 = s * w - sk * b + vt[vh:vh + 1, :] * kv
            s_ref[vh] = s2
            rows.append(jnp.sum(s2 * r, axis=0, keepdims=True))
        y_ref[t] = jnp.concatenate(rows, axis=0)
        return carry

    lax.fori_loop(0, SCAN_TC, step, 0)


def _rwkv_scan(w_t, kk_t, b_t, k_t, r_t, v_t, s0):
    t = w_t.shape[0]
    tile = pl.BlockSpec((SCAN_TC, N_A, LANE), lambda i: (i, 0, 0))
    vspec = pl.BlockSpec((SCAN_TC, V_HI, LANE), lambda i: (i, 0, 0))
    sspec = pl.BlockSpec((V_HI, N_A, LANE), lambda i: (0, 0, 0))
    return pl.pallas_call(
        _rwkv_scan_kernel,
        grid=(t // SCAN_TC,),
        in_specs=[tile] * 5 + [vspec, sspec],
        out_specs=[vspec, sspec],
        out_shape=[jax.ShapeDtypeStruct((t, V_HI, LANE), F32),
                   jax.ShapeDtypeStruct((V_HI, N_A, LANE), F32)],
        compiler_params=_cparams(("arbitrary",)),
        name="rwkv_scan",
    )(w_t, kk_t, b_t, k_t, r_t, v_t, s0)


RT_C = 128


def _retile_kernel(*refs, bsz, is_v):
    ins = refs[:bsz]
    o_ref, z_ref = refs[bsz:]
    for b in range(bsz):
        xt = ins[b][...].T
        for h in range(H_A):
            z_ref[b * H_A + h] = xt[h * N_A:(h + 1) * N_A, :]
    if is_v:
        for vh in range(V_HI):
            rows = [z_ref[:, vh * (N_A // V_HI) + vl, :] for vl in range(N_A // V_HI)]
            o_ref[:, vh, :] = jnp.concatenate(rows, axis=0).T
    else:
        reps = LANE // (bsz * H_A)
        for c in range(N_A):
            o_ref[:, c, :] = jnp.concatenate([z_ref[:, c, :]] * reps, axis=0).T


def _retile(x, bsz, t, is_v):
    nt = t // RT_C
    rows = V_HI if is_v else N_A
    return pl.pallas_call(
        functools.partial(_retile_kernel, bsz=bsz, is_v=is_v),
        grid=(nt,),
        in_specs=[pl.BlockSpec((RT_C, W_A), functools.partial(lambda i, b: (b * nt + i, 0), b=b))
                  for b in range(bsz)],
        out_specs=pl.BlockSpec((RT_C, rows, LANE), lambda i: (i, 0, 0)),
        out_shape=jax.ShapeDtypeStruct((t, rows, LANE), F32),
        scratch_shapes=[pltpu.VMEM((bsz * H_A, N_A, RT_C), F32)],
        compiler_params=_cparams(("parallel",)),
        name="rwkv_retile_v" if is_v else "rwkv_retile",
    )(*([x] * bsz))


def _to_scan_tiles(x, bsz, t):
    return _retile(x, bsz, t, False)


def _to_scan_v(v, bsz, t):
    return _retile(v, bsz, t, True)


def _from_scan_y(y, bsz, t):
    y = y.reshape(t, V_HI, N_A // V_HI, bsz, H_A).transpose(3, 0, 4, 1, 2)
    return y.reshape(bsz * t, W_A)


def _from_scan_state(s, bsz):
    s = s.reshape(V_HI, N_A, N_A // V_HI, bsz, H_A).transpose(3, 4, 0, 2, 1)
    return s.reshape(bsz, H_A, N_A, N_A)


def _rwkv_step_kernel(s_ref, w_ref, kk_ref, b_ref, k_ref, r_ref, v_ref, so_ref, y_ref):
    w = w_ref[...]
    kk = kk_ref[...]
    b = b_ref[...]
    kv = k_ref[...]
    r = r_ref[...]

    def body(v, carry):
        s = s_ref[v]
        sk = jnp.sum(s * kk, axis=0, keepdims=True)
        s2 = s * w - sk * b + v_ref[pl.ds(v, 1), :] * kv
        so_ref[v] = s2
        y_ref[pl.ds(v, 1), :] = jnp.sum(s2 * r, axis=0, keepdims=True)
        return carry

    lax.fori_loop(0, N_A, body, 0, unroll=4)


def _rwkv_step(state_t, layer, w, kk, b, k2, r, v):
    n = w.shape[0]
    tile = pl.BlockSpec((N_A, n), lambda h: (h, 0))
    return pl.pallas_call(
        _rwkv_step_kernel,
        grid=(H_A,),
        in_specs=[pl.BlockSpec((None, None, N_A, N_A, n), lambda h: (layer, h, 0, 0, 0)),
                  tile, tile, tile, tile, tile, tile],
        out_specs=[pl.BlockSpec((None, N_A, N_A, n), lambda h: (h, 0, 0, 0)), tile],
        out_shape=[jax.ShapeDtypeStruct((H_A, N_A, N_A, n), F32),
                   jax.ShapeDtypeStruct((W_A, n), F32)],
        compiler_params=_cparams(("parallel",)),
        name="rwkv_step",
    )(state_t, w.T, kk.T, b.T, k2.T, r.T, v.T)


def _lam(lq1, lk1, lq2, lk2, lam_init):
    return (jnp.exp(jnp.sum(lq1[...] * lk1[...], axis=-1, keepdims=True))
            - jnp.exp(jnp.sum(lq2[...] * lk2[...], axis=-1, keepdims=True)) + lam_init)


def _online(s, m, l):
    m_new = jnp.maximum(m, jnp.max(s, axis=-1, keepdims=True))
    alpha = jnp.exp(m - m_new)
    p = jnp.exp(s - m_new)
    return p, alpha, m_new, l * alpha + jnp.sum(p, axis=-1, keepdims=True)


def _diff_attn_kernel(q_ref, k_ref, v_ref, g_ref, sg_ref, lq1, lk1, lq2, lk2, o_ref, ko_ref, vo_ref,
                      *, lam_init, tq, tk):
    qi = pl.program_id(2)

    @pl.when(qi == 0)
    def _():
        ko_ref[...] = k_ref[...]
        vo_ref[...] = v_ref[...]

    lane = lax.broadcasted_iota(jnp.int32, (1, LANE), 1)
    q = q_ref[...] * (D_B ** -0.5)
    q1 = jnp.where(lane < D_B, q, 0.0).astype(BF16)
    q2 = jnp.where(lane >= D_B, q, 0.0).astype(BF16)
    rows = qi * tq + lax.broadcasted_iota(jnp.int32, (tq, tk), 0)
    cols0 = lax.broadcasted_iota(jnp.int32, (tq, tk), 1)

    def body(j, carry):
        m1, l1, a1, m2, l2, a2 = carry
        off = pl.multiple_of(j * tk, tk)
        kb = k_ref[pl.ds(off, tk), :].astype(BF16)
        vb = v_ref[pl.ds(off, tk), :].astype(BF16)
        mask = (cols0 + off) <= rows
        s1 = jnp.where(mask, _dot_nt(q1, kb), NEG)
        s2 = jnp.where(mask, _dot_nt(q2, kb), NEG)
        p1, al1, m1, l1 = _online(s1, m1, l1)
        p2, al2, m2, l2 = _online(s2, m2, l2)
        a1 = a1 * al1 + jnp.dot(p1.astype(BF16), vb, preferred_element_type=F32)
        a2 = a2 * al2 + jnp.dot(p2.astype(BF16), vb, preferred_element_type=F32)
        return m1, l1, a1, m2, l2, a2

    mi = jnp.full((tq, 1), NEG, F32)
    zi = jnp.zeros((tq, 1), F32)
    ai = jnp.zeros((tq, LANE), F32)
    nkv = (qi * tq + tq + tk - 1) // tk
    m1, l1, a1, m2, l2, a2 = lax.fori_loop(0, nkv, body, (mi, zi, ai, mi, zi, ai))
    lam = _lam(lq1, lk1, lq2, lk2, lam_init)
    o = a1 / l1 - lam * (a2 / l2)
    o = o * lax.rsqrt(jnp.mean(o * o, axis=-1, keepdims=True) + SUBLN_EPS) * sg_ref[...]
    o_ref[...] = o * (1.0 - lam_init) * _silu(g_ref[...])


def _diff_attn(proj, bsz, t, subln_g, lq1, lk1, lq2, lk2, lam_init, tq=512, tk=512):
    nq = t // tq
    small = pl.BlockSpec((1, D_B), lambda b, h, i: (0, 0))
    return pl.pallas_call(
        functools.partial(_diff_attn_kernel, lam_init=lam_init, tq=tq, tk=tk),
        grid=(bsz, H_B, nq),
        in_specs=[pl.BlockSpec((tq, LANE), lambda b, h, i: (b * nq + i, C_BQ // LANE + h)),
                  pl.BlockSpec((t, LANE), lambda b, h, i: (b, C_BK // LANE + h)),
                  pl.BlockSpec((t, LANE), lambda b, h, i: (b, C_BV // LANE + h)),
                  pl.BlockSpec((tq, LANE), lambda b, h, i: (b * nq + i, C_BG // LANE + h)),
                  pl.BlockSpec((1, LANE), lambda b, h, i: (0, 0)),
                  small, small, small, small],
        out_specs=[pl.BlockSpec((tq, LANE), lambda b, h, i: (b * nq + i, h)),
                   pl.BlockSpec((None, None, t, LANE), lambda b, h, i: (b, h, 0, 0)),
                   pl.BlockSpec((None, None, t, LANE), lambda b, h, i: (b, h, 0, 0))],
        out_shape=[jax.ShapeDtypeStruct((bsz * t, W_B), F32),
                   jax.ShapeDtypeStruct((bsz, H_B, t, LANE), F32),
                   jax.ShapeDtypeStruct((bsz, H_B, t, LANE), F32)],
        compiler_params=_cparams(("parallel", "parallel", "arbitrary")),
        name="diff_attn",
    )(proj, proj, proj, proj, subln_g.reshape(1, -1), lq1.reshape(1, -1), lk1.reshape(1, -1),
      lq2.reshape(1, -1), lk2.reshape(1, -1))


def _foxcum_kernel(cf_ref, bf_ref, tri_ref, lf_ref, cum_ref, cumt_ref, lft_ref, carry_ref, *, tm):
    @pl.when(pl.program_id(1) == 0)
    def _():
        carry_ref[...] = jnp.zeros_like(carry_ref)

    lf = -_softplus(-(cf_ref[...] + bf_ref[...]))
    lf_ref[...] = lf
    lft_ref[...] = lf.T[:16, :]
    c = jnp.dot(tri_ref[...], lf, precision=HI, preferred_element_type=F32) + carry_ref[...]
    cum_ref[...] = c
    carry_ref[...] = c[tm - 1:tm, :]
    cumt_ref[...] = c.T[:16, :]


def _foxcum(proj, bsz, t, bf_pad, tm):
    nt = t // tm
    tri = (jnp.arange(tm)[:, None] >= jnp.arange(tm)[None, :]).astype(F32)
    tok = pl.BlockSpec((tm, LANE), lambda b, i: (b * nt + i, 0))
    return pl.pallas_call(
        functools.partial(_foxcum_kernel, tm=tm),
        grid=(bsz, nt),
        in_specs=[pl.BlockSpec((tm, LANE), lambda b, i: (b * nt + i, C_CF // LANE)),
                  pl.BlockSpec((1, LANE), lambda b, i: (0, 0)),
                  pl.BlockSpec((tm, tm), lambda b, i: (0, 0))],
        out_specs=[tok, tok, pl.BlockSpec((None, 16, tm), lambda b, i: (b, 0, i)),
                   pl.BlockSpec((None, 16, tm), lambda b, i: (b, 0, i))],
        out_shape=[jax.ShapeDtypeStruct((bsz * t, LANE), F32),
                   jax.ShapeDtypeStruct((bsz * t, LANE), F32),
                   jax.ShapeDtypeStruct((bsz, 16, t), F32),
                   jax.ShapeDtypeStruct((bsz, 16, t), F32)],
        scratch_shapes=[pltpu.VMEM((1, LANE), F32)],
        compiler_params=_cparams(("parallel", "arbitrary")),
        name="fox_cum",
    )(proj, bf_pad, tri)


def _fox_attn_kernel(q_ref, k_ref, v_ref, g_ref, cum_ref, cumt_ref, o_ref, kto_ref, vto_ref, *, tq, tk):
    hp = pl.program_id(1)
    qi = pl.program_id(2)

    @pl.when(qi == 0)
    def _():
        kto_ref[...] = k_ref[...].T
        vto_ref[...] = v_ref[...].T

    lane = lax.broadcasted_iota(jnp.int32, (1, LANE), 1)
    q = q_ref[...] * (D_C ** -0.5)
    qa = jnp.where(lane < D_C, q, 0.0).astype(BF16)
    qb = jnp.where(lane >= D_C, q, 0.0).astype(BF16)
    cq = cum_ref[...]
    cqa = jnp.sum(jnp.where(lane == 2 * hp, cq, 0.0), axis=-1, keepdims=True)
    cqb = jnp.sum(jnp.where(lane == 2 * hp + 1, cq, 0.0), axis=-1, keepdims=True)
    rows = qi * tq + lax.broadcasted_iota(jnp.int32, (tq, tk), 0)
    cols0 = lax.broadcasted_iota(jnp.int32, (tq, tk), 1)

    def body(j, carry):
        ma, la, aa, mb, lb, ab = carry
        off = pl.multiple_of(j * tk, tk)
        kb = k_ref[pl.ds(off, tk), :].astype(BF16)
        vb = v_ref[pl.ds(off, tk), :].astype(BF16)
        cka = cumt_ref[pl.ds(2 * hp, 1), pl.ds(off, tk)]
        ckb = cumt_ref[pl.ds(2 * hp + 1, 1), pl.ds(off, tk)]
        mask = (cols0 + off) <= rows
        sa = jnp.where(mask, _dot_nt(qa, kb) + (cqa - cka), NEG)
        sb = jnp.where(mask, _dot_nt(qb, kb) + (cqb - ckb), NEG)
        pa, ala, ma, la = _online(sa, ma, la)
        pb, alb, mb, lb = _online(sb, mb, lb)
        aa = aa * ala + jnp.dot(pa.astype(BF16), vb, preferred_element_type=F32)
        ab = ab * alb + jnp.dot(pb.astype(BF16), vb, preferred_element_type=F32)
        return ma, la, aa, mb, lb, ab

    mi = jnp.full((tq, 1), NEG, F32)
    zi = jnp.zeros((tq, 1), F32)
    ai = jnp.zeros((tq, LANE), F32)
    nkv = (qi * tq + tq + tk - 1) // tk
    ma, la, aa, mb, lb, ab = lax.fori_loop(0, nkv, body, (mi, zi, ai, mi, zi, ai))
    o = jnp.where(lane < D_C, aa / la, ab / lb)
    o_ref[...] = o * _silu(g_ref[...])


def _fox_attn(proj, cum, cumt, bsz, t, tq=512, tk=512):
    nq = t // tq
    return pl.pallas_call(
        functools.partial(_fox_attn_kernel, tq=tq, tk=tk),
        grid=(bsz, H_C // 2, nq),
        in_specs=[pl.BlockSpec((tq, LANE), lambda b, h, i: (b * nq + i, C_CQ // LANE + h)),
                  pl.BlockSpec((t, LANE), lambda b, h, i: (b, C_CK // LANE + h)),
                  pl.BlockSpec((t, LANE), lambda b, h, i: (b, C_CV // LANE + h)),
                  pl.BlockSpec((tq, LANE), lambda b, h, i: (b * nq + i, C_CG // LANE + h)),
                  pl.BlockSpec((tq, LANE), lambda b, h, i: (b * nq + i, 0)),
                  pl.BlockSpec((None, 16, t), lambda b, h, i: (b, 0, 0))],
        out_specs=[pl.BlockSpec((tq, LANE), lambda b, h, i: (b * nq + i, h)),
                   pl.BlockSpec((None, None, LANE, t), lambda b, h, i: (b, h, 0, 0)),
                   pl.BlockSpec((None, None, LANE, t), lambda b, h, i: (b, h, 0, 0))],
        out_shape=[jax.ShapeDtypeStruct((bsz * t, W_C), F32),
                   jax.ShapeDtypeStruct((bsz, H_C // 2, LANE, t), F32),
                   jax.ShapeDtypeStruct((bsz, H_C // 2, LANE, t), F32)],
        compiler_params=_cparams(("parallel", "parallel", "arbitrary")),
        name="fox_attn",
    )(proj, proj, proj, proj, cum, cumt)


PP = 16
N_PAGES = 16
NSTEP = N_PAGES // PP
DEC_GROUP = 4


def _diff_dec_kernel(pt_ref, q_ref, kn_ref, vn_ref, g_ref, sg_ref, lq1, lk1, lq2, lk2, *rest,
                     lam_init):
    kp = rest[:PP]
    vp = rest[PP:2 * PP]
    o_ref, qrep_ref, m_ref, l_ref, acc_ref = rest[2 * PP:]
    s = pl.program_id(1)
    lo = lax.broadcasted_iota(jnp.int32, (1, LANE), 1) < D_B

    def spread(x):
        sw = pltpu.roll(x, D_B, axis=1)
        return jnp.where(lo, x, sw), jnp.where(lo, sw, x)

    @pl.when(s == 0)
    def _():
        qrow = q_ref[...] * (D_B ** -0.5)
        r = lax.broadcasted_iota(jnp.int32, (LANE, LANE), 0)
        c = lax.broadcasted_iota(jnp.int32, (LANE, LANE), 1)
        for h in range(H_B):
            qh = qrow[:, h * LANE:(h + 1) * LANE]
            col = jnp.sum(jnp.where(r == c, qh, 0.0), axis=1, keepdims=True)
            same_map = (r < D_B) == (c < D_B)
            qrep_ref[h] = jnp.where(same_map, jnp.broadcast_to(col, (LANE, LANE)), 0.0).astype(BF16)
        m_ref[...] = jnp.full_like(m_ref, NEG)
        l_ref[...] = jnp.zeros_like(l_ref)
        acc_ref[...] = jnp.zeros_like(acc_ref)

    for h in range(H_B):
        w = qrep_ref[h]
        m_run = m_ref[h][0:1, :]
        acc1 = acc_ref[2 * h]
        acc2 = acc_ref[2 * h + 1]
        l = l_ref[h]
        for g0 in range(0, PP, DEC_GROUP):
            pages = range(g0, g0 + DEC_GROUP)
            ss = [jnp.dot(kp[j][h].astype(BF16), w, preferred_element_type=F32) for j in pages]
            mx = ss[0]
            for x in ss[1:]:
                mx = jnp.maximum(mx, x)
            m_new = jnp.maximum(m_run, jnp.max(mx, axis=0, keepdims=True))
            alpha = jnp.exp(m_run - m_new)
            al1, al2 = spread(alpha)
            acc1 = acc1 * al1
            acc2 = acc2 * al2
            l = l * alpha
            for x, j in zip(ss, pages):
                p = jnp.exp(x - m_new)
                p1, p2 = spread(p)
                v = vp[j][h]
                acc1 = acc1 + p1 * v
                acc2 = acc2 + p2 * v
                l = l + p
            m_run = m_new
        acc_ref[2 * h] = acc1
        acc_ref[2 * h + 1] = acc2
        l_ref[h] = l
        m_ref[h] = jnp.broadcast_to(m_run, (8, LANE))

    @pl.when(s == NSTEP - 1)
    def _():
        lam = _lam(lq1, lk1, lq2, lk2, lam_init)
        for h in range(H_B):
            knh = jnp.broadcast_to(kn_ref[:, h * LANE:(h + 1) * LANE], (8, LANE)).astype(BF16)
            s_new = jnp.dot(knh, qrep_ref[h], preferred_element_type=F32)[0:1, :]
            vnh = vn_ref[:, h * LANE:(h + 1) * LANE]
            m_old = m_ref[h][0:1, :]
            m_f = jnp.maximum(m_old, s_new)
            alpha = jnp.exp(m_old - m_f)
            pn = jnp.exp(s_new - m_f)
            den = jnp.sum(l_ref[h], axis=0, keepdims=True) * alpha + pn
            als, pns, dens = spread(alpha), spread(pn), spread(den)
            outs = [(jnp.sum(acc_ref[2 * h + mp], axis=0, keepdims=True) * als[mp] + pns[mp] * vnh)
                    / dens[mp] for mp in range(2)]
            od = outs[0] - lam * outs[1]
            od = od * lax.rsqrt(jnp.mean(od * od, axis=-1, keepdims=True) + SUBLN_EPS) * sg_ref[...]
            gate = g_ref[:, h * LANE:(h + 1) * LANE]
            o_ref[:, h * LANE:(h + 1) * LANE] = od * (1.0 - lam_init) * _silu(gate)


def _diff_decode(page_flat, q, kn, vn, g, cache_k, cache_v, layer, subln_g, lq1, lk1, lq2, lk2,
                 lam_init):
    n = q.shape[0]
    full = pl.BlockSpec((None, 1, W_B), lambda b, s, pt: (b, 0, 0))
    small = pl.BlockSpec((1, D_B), lambda b, s, pt: (0, 0))
    seq = lambda x: x.reshape(n, 1, W_B)

    def page_spec(j):
        return pl.BlockSpec((None, None, H_B, PAGE, 2 * D_B),
                            lambda b, s, pt: (layer, pt[b * N_PAGES + s * PP + j], 0, 0, 0))

    grid_spec = pltpu.PrefetchScalarGridSpec(
        num_scalar_prefetch=1,
        grid=(n, NSTEP),
        in_specs=[full, full, full, full, pl.BlockSpec((1, LANE), lambda b, s, pt: (0, 0)),
                  small, small, small, small]
                 + [page_spec(j) for j in range(PP)] + [page_spec(j) for j in range(PP)],
        out_specs=full,
        scratch_shapes=[pltpu.VMEM((H_B, LANE, LANE), BF16),
                        pltpu.VMEM((H_B, 8, LANE), F32),
                        pltpu.VMEM((H_B, PAGE, LANE), F32),
                        pltpu.VMEM((2 * H_B, PAGE, LANE), F32)],
    )
    return pl.pallas_call(
        functools.partial(_diff_dec_kernel, lam_init=lam_init),
        grid_spec=grid_spec,
        out_shape=jax.ShapeDtypeStruct((n, 1, W_B), F32),
        compiler_params=_cparams(("arbitrary", "arbitrary")),
        name="diff_decode",
    )(page_flat, seq(q), seq(kn), seq(vn), seq(g), subln_g.reshape(1, -1), lq1.reshape(1, -1),
      lk1.reshape(1, -1), lq2.reshape(1, -1), lk2.reshape(1, -1),
      *([cache_k] * PP), *([cache_v] * PP)).reshape(n, W_B)


def _fox_dec_kernel(pt_ref, q_ref, kn_ref, vn_ref, g_ref, lfn_ref, tri_ref, *rest):
    kp = rest[:PP]
    vp = rest[PP:2 * PP]
    lfp = rest[2 * PP:3 * PP]
    o_ref, qcol_ref, m_ref, l_ref, c_ref, acc_ref = rest[3 * PP:]
    b = pl.program_id(0)
    s = pl.program_id(1)

    @pl.when(s == 0)
    def _():
        qrow = q_ref[...] * (D_C ** -0.5)
        r = lax.broadcasted_iota(jnp.int32, (D_C, D_C), 0)
        c = lax.broadcasted_iota(jnp.int32, (D_C, D_C), 1)
        for h in range(H_C):
            qh = qrow[:, h * D_C:(h + 1) * D_C]
            col = jnp.sum(jnp.where(r == c, qh, 0.0), axis=1, keepdims=True)
            qcol_ref[h] = jnp.broadcast_to(col, (D_C, PAGE))
        m_ref[...] = jnp.full_like(m_ref, NEG)
        l_ref[...] = jnp.zeros_like(l_ref)
        c_ref[...] = jnp.zeros_like(c_ref)
        acc_ref[...] = jnp.zeros_like(acc_ref)

    tri = tri_ref[...]
    carry = c_ref[...][:, :1]
    row16 = lax.broadcasted_iota(jnp.int32, (16, PAGE), 0)
    tiles = []
    for j in range(PP):
        sub = pt_ref[b * N_PAGES + s * PP + j] % 8
        sc_t = jnp.zeros((16, PAGE), F32)
        lf = jnp.zeros((16, PAGE), F32)
        for h in range(H_C):
            sh = jnp.sum(kp[j][h] * qcol_ref[h], axis=0, keepdims=True)
            sc_t = jnp.where(row16 == h, sh, sc_t)
            lf = jnp.where(row16 == h, lfp[j][h, pl.ds(sub, 1), :], lf)
        cum = jnp.dot(lf, tri, precision=HI, preferred_element_type=F32) + carry
        carry = carry + jnp.sum(lf, axis=-1, keepdims=True)
        tiles.append(sc_t - cum)
    sc = jnp.concatenate(tiles, axis=1)
    p, alpha, m_new, l_new = _online(sc, m_ref[...][:, :1], l_ref[...][:, :1])
    m_ref[...] = jnp.broadcast_to(m_new, (16, LANE))
    l_ref[...] = jnp.broadcast_to(l_new, (16, LANE))
    c_ref[...] = jnp.broadcast_to(carry, (16, LANE))
    for h in range(H_C):
        acc = acc_ref[h] * alpha[h:h + 1, :]
        for j in range(PP):
            acc = acc + vp[j][h] * p[h:h + 1, j * PAGE:(j + 1) * PAGE]
        acc_ref[h] = acc

    @pl.when(s == NSTEP - 1)
    def _():
        qrow = q_ref[...] * (D_C ** -0.5)
        rcol = lax.broadcasted_iota(jnp.int32, (16, 1), 0)
        s_new = jnp.zeros((16, 1), F32)
        for h in range(H_C):
            dot = jnp.sum(qrow[:, h * D_C:(h + 1) * D_C] * kn_ref[:, h * D_C:(h + 1) * D_C],
                          axis=-1, keepdims=True)
            s_new = jnp.where(rcol == h, dot, s_new)
        s_new = s_new - (carry + lfn_ref[...][:, :1])
        m_f = jnp.maximum(m_new, s_new)
        al = jnp.exp(m_new - m_f)
        pn = jnp.exp(s_new - m_f)
        l_f = l_new * al + pn
        ones = jnp.ones((8, PAGE), F32)
        tot = lax.dot_general(ones, acc_ref[...].reshape(H_C * D_C, PAGE), (((1,), (1,)), ((), ())),
                              precision=HI, preferred_element_type=F32)[0:1, :]
        outs = []
        for h in range(H_C):
            past = tot[:, h * D_C:(h + 1) * D_C]
            vnh = vn_ref[:, h * D_C:(h + 1) * D_C]
            outs.append((past * al[h:h + 1, :] + pn[h:h + 1, :] * vnh) / l_f[h:h + 1, :])
        o_all = jnp.concatenate(outs, axis=1)
        o_ref[...] = o_all * _silu(g_ref[...])


def _fox_decode(page_flat, q, kn, vn, g, lfn_rep, cache_k, cache_v, cache_lft, layer):
    n = q.shape[0]
    full = pl.BlockSpec((None, 1, W_C), lambda b, s, pt: (b, 0, 0))
    seq = lambda x: x.reshape(n, 1, W_C)
    tri = (jnp.arange(PAGE)[:, None] <= jnp.arange(PAGE)[None, :]).astype(F32)

    def page_spec(j):
        return pl.BlockSpec((None, None, H_C, D_C, PAGE),
                            lambda b, s, pt: (layer, pt[b * N_PAGES + s * PP + j], 0, 0, 0))

    def lf_spec(j):
        return pl.BlockSpec((None, H_C, None, 8, PAGE),
                            lambda b, s, pt: (layer, 0, pt[b * N_PAGES + s * PP + j] // 8, 0, 0))

    grid_spec = pltpu.PrefetchScalarGridSpec(
        num_scalar_prefetch=1,
        grid=(n, NSTEP),
        in_specs=[full, full, full, full,
                  pl.BlockSpec((None, 16, LANE), lambda b, s, pt: (b, 0, 0)),
                  pl.BlockSpec((PAGE, PAGE), lambda b, s, pt: (0, 0))]
                 + [page_spec(j) for j in range(PP)] + [page_spec(j) for j in range(PP)]
                 + [lf_spec(j) for j in range(PP)],
        out_specs=full,
        scratch_shapes=[pltpu.VMEM((H_C, D_C, PAGE), F32)] + [pltpu.VMEM((16, LANE), F32)] * 3
                       + [pltpu.VMEM((H_C, D_C, PAGE), F32)],
    )
    return pl.pallas_call(
        _fox_dec_kernel,
        grid_spec=grid_spec,
        out_shape=jax.ShapeDtypeStruct((n, 1, W_C), F32),
        compiler_params=_cparams(("arbitrary", "arbitrary")),
        name="fox_decode",
    )(page_flat, seq(q), seq(kn), seq(vn), seq(g), lfn_rep, tri, *([cache_k] * PP),
      *([cache_v] * PP), *([cache_lft] * PP)).reshape(n, W_C)


def _merge_kernel(ga_ref, gb_ref, gc_ref, ag_ref, y_ref, bonus_ref, gng_ref, gnb_ref, bd_ref,
                  yb_ref, yc_ref, w_ref, o_ref):
    y = y_ref[...]
    bd = bd_ref[...]
    mean = jnp.dot(y, bd, precision=HI, preferred_element_type=F32) * (1.0 / N_A)
    d = y - mean
    var = jnp.dot(d * d, bd, precision=HI, preferred_element_type=F32) * (1.0 / N_A)
    yn = d * lax.rsqrt(var + GN_EPS) * gng_ref[...] + gnb_ref[...] + bonus_ref[...]
    ya = yn * _silu(ag_ref[...])
    pa = jnp.dot(ya.astype(BF16), w_ref[0:W_A, :], preferred_element_type=F32)
    pb = jnp.dot(yb_ref[...].astype(BF16), w_ref[W_A:W_A + W_B, :], preferred_element_type=F32)
    pc = jnp.dot(yc_ref[...].astype(BF16), w_ref[W_A + W_B:, :], preferred_element_type=F32)
    merged = _sigmoid(ga_ref[...]) * pa + _sigmoid(gb_ref[...]) * pb + _sigmoid(gc_ref[...]) * pc
    o_ref[...] = merged.astype(BF16)


def _merge(proj, y_raw, bonus, gn_g, gn_b, bd, yb, yc, w_br, tm):
    m = proj.shape[0]
    gate = lambda c: pl.BlockSpec((tm, D_MODEL), lambda i: (i, c))
    tok = lambda n: pl.BlockSpec((tm, n), lambda i: (i, 0))
    row = pl.BlockSpec((1, W_A), lambda i: (0, 0))
    return pl.pallas_call(
        _merge_kernel,
        grid=(m // tm,),
        in_specs=[gate(0), gate(1), gate(2),
                  pl.BlockSpec((tm, W_A), lambda i: (i, C_AGATE // W_A)),
                  tok(W_A), tok(W_A), row, row, pl.BlockSpec((W_A, W_A), lambda i: (0, 0)),
                  tok(W_B), tok(W_C), pl.BlockSpec((D_MODEL, D_MODEL), lambda i: (0, 0))],
        out_specs=tok(D_MODEL),
        out_shape=jax.ShapeDtypeStruct((m, D_MODEL), BF16),
        compiler_params=_cparams(("parallel",)),
        name="merge",
    )(proj, proj, proj, proj, y_raw, bonus, gn_g.reshape(1, -1), gn_b.reshape(1, -1), bd, yb, yc, w_br)


def _outproj_kernel(x_ref, m_ref, w_ref, o_ref):
    o_ref[...] = x_ref[...] + jnp.dot(m_ref[...], w_ref[...], preferred_element_type=F32)


def _outproj(x2d, merged, w_out, tm):
    m = x2d.shape[0]
    tok = pl.BlockSpec((tm, D_MODEL), lambda i: (i, 0))
    return pl.pallas_call(
        _outproj_kernel,
        grid=(m // tm,),
        in_specs=[tok, tok, pl.BlockSpec((D_MODEL, D_MODEL), lambda i: (0, 0))],
        out_specs=tok,
        out_shape=jax.ShapeDtypeStruct((m, D_MODEL), F32),
        compiler_params=_cparams(("parallel",)),
        name="outproj",
    )(x2d, merged, w_out)


def _rmsnorm_kernel(x_ref, g_ref, o_ref):
    x = x_ref[...]
    o_ref[...] = x * lax.rsqrt(jnp.mean(x * x, axis=-1, keepdims=True) + NORM_EPS) * g_ref[...]


def _rmsnorm(x2d, g, tm):
    m = x2d.shape[0]
    tok = pl.BlockSpec((tm, D_MODEL), lambda i: (i, 0))
    return pl.pallas_call(
        _rmsnorm_kernel,
        grid=(m // tm,),
        in_specs=[tok, pl.BlockSpec((1, D_MODEL), lambda i: (0, 0))],
        out_specs=tok,
        out_shape=jax.ShapeDtypeStruct((m, D_MODEL), F32),
        compiler_params=_cparams(("parallel",)),
        name="final_norm",
    )(x2d, g.reshape(1, -1))


def _w_row_offsets():
    segs = [(8332, 6144), (1664, 512), (0, 1664), (2176, 5376), (7564, 768), (7552, LANE)]
    return [start + k for start, size in segs for k in range(0, size, LANE)]


def _wperm_kernel(offs_ref, w_ref, o0_ref, o1_ref):
    o0_ref[...] = w_ref[:, 0, :].astype(BF16)
    o1_ref[...] = w_ref[:, 1, :].astype(BF16)


def _permute_w_in(w_in_t):
    assert w_in_t.shape[1] == 2
    offs = jnp.asarray(_w_row_offsets(), jnp.int32)
    grid_spec = pltpu.PrefetchScalarGridSpec(
        num_scalar_prefetch=1,
        grid=(NP // LANE,),
        in_specs=[pl.BlockSpec((pl.Element(LANE), pl.Element(2), pl.Element(D_MODEL)),
                               lambda d, offs: (offs[d], 0, 0))],
        out_specs=[pl.BlockSpec((LANE, D_MODEL), lambda d, offs: (d, 0))] * 2,
    )
    return pl.pallas_call(
        _wperm_kernel,
        grid_spec=grid_spec,
        out_shape=[jax.ShapeDtypeStruct((NP, D_MODEL), BF16)] * 2,
        compiler_params=_cparams(("arbitrary",)),
        name="w_in_permute",
    )(offs, w_in_t)


def kernel(x_prompt, x_sample, cache_diff_k, cache_diff_v, cache_fox_k, cache_fox_v, cache_fox_logf,
           state_rwkv_wkv, state_rwkv_shift, page_table, norm_g, w_in, rwkv_mu, rwkv_w0, rwkv_w2,
           rwkv_a0, rwkv_a2, rwkv_k_k, rwkv_k_a, rwkv_r_k, rwkv_gn_g, rwkv_gn_b, diff_lq1, diff_lk1,
           diff_lq2, diff_lk2, diff_subln_g, fox_b_f, w_branch, w_out, final_g):
    bsz, t, _ = x_prompt.shape
    n = x_sample.shape[0]
    mp = bsz * t
    hp = x_prompt.reshape(mp, D_MODEL)
    hs = x_sample.reshape(n, D_MODEL)
    page_flat = page_table.reshape(-1).astype(jnp.int32)
    bd = jnp.kron(jnp.eye(H_A, dtype=F32), jnp.ones((N_A, N_A), F32))
    n_pool = cache_fox_logf.shape[1]
    assert n_pool % 8 == 0 and n == LANE
    dk_all = jnp.transpose(cache_diff_k, (0, 1, 3, 2, 4))
    dv_all = jnp.transpose(cache_diff_v, (0, 1, 3, 2, 4))
    fk_all = jnp.transpose(cache_fox_k, (0, 1, 3, 4, 2))
    fv_all = jnp.transpose(cache_fox_v, (0, 1, 3, 4, 2))
    lft_all = jnp.transpose(cache_fox_logf, (0, 3, 1, 2)).reshape(DEPTH, H_C, n_pool // 8, 8, PAGE)
    state_t = jnp.transpose(state_rwkv_wkv, (0, 2, 3, 4, 1))
    zero_lora = jnp.zeros((64, W_A), BF16)
    w_p_all = _permute_w_in(jnp.transpose(w_in, (2, 0, 1)))
    outs_p, outs_s = [], []
    for l in range(DEPTH):
        lam_init = 0.8 - 0.6 * math.exp(-0.3 * l)
        w_p = w_p_all[l]
        w2p = jnp.concatenate([rwkv_w2[l].astype(BF16), zero_lora], axis=0)
        a2p = jnp.concatenate([zero_lora, rwkv_a2[l].astype(BF16)], axis=0)
        bf_pad = jnp.pad(fox_b_f[l], (0, LANE - H_C)).reshape(1, LANE)
        w_br = w_branch[l].astype(BF16)
        w_o = w_out[l].astype(BF16)
        rk = rwkv_r_k[l].reshape(-1)

        proj = _inproj(hp, norm_g[l], w_p, tm=1024)
        r, dec, k2, v, kk, bb, bonus = _rwkv_prep(
            proj, None, rwkv_mu[l], rwkv_w0[l], rwkv_a0[l], rwkv_k_k[l],
            rwkv_k_a[l], rk, w2p, a2p, bd, tm=512, seq_len=t)
        y_t, s_fin = _rwkv_scan(
            _to_scan_tiles(dec, bsz, t), _to_scan_tiles(kk, bsz, t), _to_scan_tiles(bb, bsz, t),
            _to_scan_tiles(k2, bsz, t), _to_scan_tiles(r, bsz, t), _to_scan_v(v, bsz, t),
            jnp.zeros((V_HI, N_A, LANE), F32))
        y_raw = _from_scan_y(y_t, bsz, t)
        yb, dk_new, dv_new = _diff_attn(proj, bsz, t, diff_subln_g[l], diff_lq1[l], diff_lk1[l],
                                        diff_lq2[l], diff_lk2[l], lam_init)
        _, cum, cumt, lft = _foxcum(proj, bsz, t, bf_pad, tm=256)
        yc, fk_new, fv_new = _fox_attn(proj, cum, cumt, bsz, t)
        merged = _merge(proj, y_raw, bonus, rwkv_gn_g[l], rwkv_gn_b[l], bd, yb, yc, w_br, tm=256)
        hp = _outproj(hp, merged, w_o, tm=512)
        outs_p.append((
            jnp.transpose(dk_new, (0, 2, 1, 3)),
            jnp.transpose(dv_new, (0, 2, 1, 3)),
            jnp.transpose(fk_new.reshape(bsz, H_C, D_C, t), (0, 3, 1, 2)),
            jnp.transpose(fv_new.reshape(bsz, H_C, D_C, t), (0, 3, 1, 2)),
            jnp.transpose(lft[:, :H_C, :], (0, 2, 1)),
            _from_scan_state(s_fin, bsz),
            proj.reshape(bsz, t, NP)[:, t - 1, C_U:C_U + A_SHIFT_W]))

        projs = _inproj(hs, norm_g[l], w_p, tm=n)
        us = projs[:, C_U:C_U + A_SHIFT_W]
        r, dec, k2, v, kk, bb, bonus = _rwkv_prep(
            projs, state_rwkv_shift[l], rwkv_mu[l], rwkv_w0[l], rwkv_a0[l], rwkv_k_k[l],
            rwkv_k_a[l], rk, w2p, a2p, bd, tm=n)
        s_new, y_t = _rwkv_step(state_t, l, dec, kk, bb, k2, r, v)
        y_raw = y_t.T
        lfs, _, _, _ = _foxcum(projs, 1, n, bf_pad, tm=n)
        q_b = projs[:, C_BQ:C_BQ + W_B]
        k_b = projs[:, C_BK:C_BK + W_B]
        v_b = projs[:, C_BV:C_BV + W_B]
        g_b = projs[:, C_BG:C_BG + W_B]
        yb = _diff_decode(page_flat, q_b, k_b, v_b, g_b, dk_all, dv_all, l,
                          diff_subln_g[l], diff_lq1[l], diff_lk1[l], diff_lq2[l], diff_lk2[l], lam_init)
        q_c = projs[:, C_CQ:C_CQ + W_C]
        k_c = projs[:, C_CK:C_CK + W_C]
        v_c = projs[:, C_CV:C_CV + W_C]
        g_c = projs[:, C_CG:C_CG + W_C]
        lfn_rep = jnp.broadcast_to(lfs[:, :16, None], (n, 16, LANE))
        yc = _fox_decode(page_flat, q_c, k_c, v_c, g_c, lfn_rep, fk_all, fv_all, lft_all, l)
        merged = _merge(projs, y_raw, bonus, rwkv_gn_g[l], rwkv_gn_b[l], bd, yb, yc, w_br, tm=n)
        hs = _outproj(hs, merged, w_o, tm=n)
        outs_s.append((
            k_b.reshape(n, 1, H_B, 2 * D_B), v_b.reshape(n, 1, H_B, 2 * D_B),
            k_c.reshape(n, 1, H_C, D_C), v_c.reshape(n, 1, H_C, D_C),
            lfs[:, :H_C].reshape(n, 1, H_C),
            jnp.transpose(s_new, (3, 0, 1, 2)),
            us))

    y_prompt = _rmsnorm(hp, final_g, tm=512).reshape(bsz, t, D_MODEL)
    y_sample = _rmsnorm(hs, final_g, tm=n).reshape(n, 1, D_MODEL)
    st_p = [jnp.stack([o[i] for o in outs_p]) for i in range(7)]
    st_s = [jnp.stack([o[i] for o in outs_s]) for i in range(7)]
    return (y_prompt, y_sample, st_p[0], st_s[0], st_p[1], st_s[1], st_p[2], st_s[2], st_p[3], st_s[3],
            st_p[4], st_s[4], st_p[5], st_s[5], st_p[6], st_s[6])
```
